```python
import math
import jax, jax.numpy as jnp
from jax import lax
import numpy as np

D_MODEL = 4096
BATCH = 1
SEQ = 16384
DEPTH = 2

D_FF = 11008
ALPHA = (2 * DEPTH) ** 0.25
BETA = (8 * DEPTH) ** -0.25
LN_EPS = 1e-5
RMS_EPS = 1e-6
Q_BLOCK = 128
FOX_HEADS = 24
FOX_HEAD_DIM = 128
FOX_WIDTH = FOX_HEADS * FOX_HEAD_DIM
POOL_WINDOWS = (2, 4, 8, 16)
POOL_GROUPS = 4
POOL_GROUP_DIM = 256
POOL_WIDTH = POOL_GROUPS * POOL_GROUP_DIM
IN0_WIDTH = 3 * FOX_WIDTH + FOX_HEADS + POOL_WIDTH
MIX0_WIDTH = FOX_WIDTH + POOL_WIDTH
CONV_WIDTH = 1024
CONV_TAPS = 31
MLA_HEADS = 24
MLA_Q_RANK = 1024
MLA_KV_RANK = 512
MLA_NOPE_DIM = 128
MLA_ROPE_DIM = 64
MLA_V_DIM = 128
ROPE_BASE = 10000.0
IN1_WIDTH = 2 * CONV_WIDTH + MLA_Q_RANK + MLA_KV_RANK + MLA_ROPE_DIM
MIX1_WIDTH = CONV_WIDTH + MLA_HEADS * MLA_V_DIM

kernel_name = "fox_pool_conformer_mla_deepnorm_hybrid"


def layer_norm(x, g, b):
    xf = x.astype(jnp.float32)
    mu = jnp.mean(xf, axis=-1, keepdims=True)
    var = jnp.mean(jnp.square(xf - mu), axis=-1, keepdims=True)
    return ((xf - mu) * lax.rsqrt(var + LN_EPS) * g + b).astype(x.dtype)


def rms_norm(x, g):
    xf = x.astype(jnp.float32)
    return (xf * lax.rsqrt(jnp.mean(jnp.square(xf), axis=-1, keepdims=True) + RMS_EPS) * g).astype(x.dtype)


def swiglu(x, w1, w3, w2):
    return (jax.nn.silu(x @ w1) * (x @ w3)) @ w2


def causal_block_attention(q, k, v, c=None):
    B, S, H, Dk = q.shape
    Dv = v.shape[-1]
    nb = S // Q_BLOCK
    scale = Dk ** -0.5
    key_pos = jnp.arange(S)
    starts = jnp.arange(nb) * Q_BLOCK
    qb = q.reshape(B, nb, Q_BLOCK, H, Dk).transpose(1, 0, 2, 3, 4)
    ct = None if c is None else c.transpose(0, 2, 1)

    def block(qi, start, ci):
        s = jnp.einsum('bqhd,bkhd->bhqk', qi, k).astype(jnp.float32) * scale
        if ci is not None:
            s = s + ci[..., None] - ct[:, :, None, :]
        qpos = start + jnp.arange(Q_BLOCK)
        mask = qpos[:, None] >= key_pos[None, :]
        s = jnp.where(mask, s, -jnp.inf)
        p = jax.nn.softmax(s, axis=-1).astype(v.dtype)
        return jnp.einsum('bhqk,bkhd->bqhd', p, v)

    if c is None:
        out = lax.map(lambda a: block(a[0], a[1], None), (qb, starts))
    else:
        cb = ct.reshape(B, H, nb, Q_BLOCK).transpose(2, 0, 1, 3)
        out = lax.map(lambda a: block(a[0], a[1], a[2]), (qb, starts, cb))
    return out.transpose(1, 0, 2, 3, 4).reshape(B, S, H * Dv)


def multiscale_pool(u, pool_w, pool_scale):
    B, S, _ = u.shape
    uf = u.astype(jnp.float32)
    cs = jnp.concatenate([jnp.zeros((B, 1, POOL_WIDTH), jnp.float32), jnp.cumsum(uf, axis=1)], axis=1)
    t = jnp.arange(S, dtype=jnp.float32)
    outs = []
    for g, w in enumerate(POOL_WINDOWS):
        sl = slice(g * POOL_GROUP_DIM, (g + 1) * POOL_GROUP_DIM)
        csg = cs[:, :, sl]
        lagged = jnp.pad(csg, ((0, 0), (w - 1, 0), (0, 0)))[:, :S]
        count = jnp.minimum(t + 1.0, float(w))[None, :, None]
        outs.append((csg[:, 1:] - lagged) / count - uf[:, :, sl])
    pooled = jnp.stack(outs, axis=2).astype(u.dtype)
    mixed = jnp.einsum('bsgc,gcd->bsgd', pooled, pool_w).reshape(B, S, POOL_WIDTH)
    return mixed * pool_scale


def conformer_conv(u, conv_w, conv_b, ln_g, ln_b):
    a, gate = jnp.split(u, 2, axis=-1)
    h = a * jax.nn.sigmoid(gate)
    h = lax.conv_general_dilated(h, conv_w, window_strides=(1,), padding=[(CONV_TAPS - 1, 0)],
                                 dimension_numbers=('NWC', 'WIO', 'NWC'),
                                 feature_group_count=CONV_WIDTH) + conv_b
    return jax.nn.silu(layer_norm(h, ln_g, ln_b))


def rope(x, positions):
    half = MLA_ROPE_DIM // 2
    inv_freq = ROPE_BASE ** (-jnp.arange(half, dtype=jnp.float32) / half)
    ang = positions.astype(jnp.float32)[..., None] * inv_freq
    cos = jnp.cos(ang)[:, :, None, :]
    sin = jnp.sin(ang)[:, :, None, :]
    xf = x.astype(jnp.float32)
    x1, x2 = xf[..., :half], xf[..., half:]
    return jnp.concatenate([x1 * cos - x2 * sin, x1 * sin + x2 * cos], axis=-1).astype(x.dtype)


def mla(c_q, c_kv, k_pe, positions, q_norm_g, w_uq, kv_norm_g, w_ukv):
    B, S, _ = c_q.shape
    q = (rms_norm(c_q, q_norm_g) @ w_uq).reshape(B, S, MLA_HEADS, MLA_NOPE_DIM + MLA_ROPE_DIM)
    q = jnp.concatenate([q[..., :MLA_NOPE_DIM], rope(q[..., MLA_NOPE_DIM:], positions)], axis=-1)
    kv = (rms_norm(c_kv, kv_norm_g) @ w_ukv).reshape(B, S, MLA_HEADS, MLA_NOPE_DIM + MLA_V_DIM)
    k_nope, v = kv[..., :MLA_NOPE_DIM], kv[..., MLA_NOPE_DIM:]
    k_rot = rope(k_pe[:, :, None, :], positions)
    k = jnp.concatenate([k_nope, jnp.broadcast_to(k_rot, (B, S, MLA_HEADS, MLA_ROPE_DIM))], axis=-1)
    return causal_block_attention(q, k, v)


def mixer_fox_pool(x, w_in, b_f, pool_w, pool_scale, w_out):
    B, S, _ = x.shape
    h = x @ w_in
    q, k, v, f_logit, u = jnp.split(
        h, [FOX_WIDTH, 2 * FOX_WIDTH, 3 * FOX_WIDTH, 3 * FOX_WIDTH + FOX_HEADS], axis=-1)
    shp = (B, S, FOX_HEADS, FOX_HEAD_DIM)
    log_f = jax.nn.log_sigmoid((f_logit + b_f).astype(jnp.float32))
    c = jnp.cumsum(log_f, axis=1)
    y_a = causal_block_attention(q.reshape(shp), k.reshape(shp), v.reshape(shp), c)
    y_b = multiscale_pool(u, pool_w, pool_scale)
    return jnp.concatenate([y_a, y_b], axis=-1) @ w_out


def mixer_conv_mla(x, positions, w_in, conv_w, conv_b, conv_ln_g, conv_ln_b,
                   q_norm_g, w_uq, kv_norm_g, w_ukv, w_out):
    h = x @ w_in
    u_c, c_q, c_kv, k_pe = jnp.split(
        h, [2 * CONV_WIDTH, 2 * CONV_WIDTH + MLA_Q_RANK, 2 * CONV_WIDTH + MLA_Q_RANK + MLA_KV_RANK], axis=-1)
    y_c = conformer_conv(u_c, conv_w, conv_b, conv_ln_g, conv_ln_b)
    y_d = mla(c_q, c_kv, k_pe, positions, q_norm_g, w_uq, kv_norm_g, w_ukv)
    return jnp.concatenate([y_c, y_d], axis=-1) @ w_out


def setup_inputs(seed: int = 0) -> dict:
    key = jax.random.key(seed)
    ks = iter(jax.random.split(key, 64))

    def normal(shape, std):
        return std * jax.random.normal(next(ks), shape, jnp.float32)

    def dense(fan_in, shape, scale=1.0):
        return normal(shape, scale * fan_in ** -0.5)

    def gain(n):
        return 1.0 + normal((n,), 0.02)

    def bias(n):
        return normal((n,), 0.02)

    inp = {}
    inp['x'] = normal((BATCH, SEQ, D_MODEL), 1.0)
    inp['positions'] = jnp.broadcast_to(jnp.arange(SEQ, dtype=jnp.int32)[None, :], (BATCH, SEQ))
    for l in range(DEPTH):
        p = 'l%d_' % l
        inp[p + 'ffn1_w1'] = dense(D_MODEL, (D_MODEL, D_FF))
        inp[p + 'ffn1_w3'] = dense(D_MODEL, (D_MODEL, D_FF))
        inp[p + 'ffn1_w2'] = dense(D_FF, (D_FF, D_MODEL), BETA)
        inp[p + 'ln_ffn1_g'] = gain(D_MODEL)
        inp[p + 'ln_ffn1_b'] = bias(D_MODEL)
        if l % 2 == 0:
            inp[p + 'w_in'] = dense(D_MODEL, (D_MODEL, IN0_WIDTH))
            inp[p + 'b_f'] = 2.0 + normal((FOX_HEADS,), 0.5)
            inp[p + 'pool_w'] = dense(POOL_GROUP_DIM, (POOL_GROUPS, POOL_GROUP_DIM, POOL_GROUP_DIM))
            inp[p + 'pool_scale'] = gain(POOL_WIDTH)
            inp[p + 'w_out'] = dense(MIX0_WIDTH, (MIX0_WIDTH, D_MODEL), BETA)
        else:
            inp[p + 'w_in'] = dense(D_MODEL, (D_MODEL, IN1_WIDTH))
            inp[p + 'conv_w'] = dense(CONV_TAPS, (CONV_TAPS, 1, CONV_WIDTH))
            inp[p + 'conv_b'] = bias(CONV_WIDTH)
            inp[p + 'conv_ln_g'] = gain(CONV_WIDTH)
            inp[p + 'conv_ln_b'] = bias(CONV_WIDTH)
            inp[p + 'q_norm_g'] = gain(MLA_Q_RANK)
            inp[p + 'w_uq'] = dense(MLA_Q_RANK, (MLA_Q_RANK, MLA_HEADS * (MLA_NOPE_DIM + MLA_ROPE_DIM)))
            inp[p + 'kv_norm_g'] = gain(MLA_KV_RANK)
            inp[p + 'w_ukv'] = dense(MLA_KV_RANK, (MLA_KV_RANK, MLA_HEADS * (MLA_NOPE_DIM + MLA_V_DIM)))
            inp[p + 'w_out'] = dense(MIX1_WIDTH, (MIX1_WIDTH, D_MODEL), BETA)
        inp[p + 'ln_mix_g'] = gain(D_MODEL)
        inp[p + 'ln_mix_b'] = bias(D_MODEL)
        inp[p + 'ffn2_w1'] = dense(D_MODEL, (D_MODEL, D_FF))
        inp[p + 'ffn2_w3'] = dense(D_MODEL, (D_MODEL, D_FF))
        inp[p + 'ffn2_w2'] = dense(D_FF, (D_FF, D_MODEL), BETA)
        inp[p + 'ln_ffn2_g'] = gain(D_MODEL)
        inp[p + 'ln_ffn2_b'] = bias(D_MODEL)
    return inp


def reference(x, positions,
              l0_ffn1_w1, l0_ffn1_w3, l0_ffn1_w2, l0_ln_ffn1_g, l0_ln_ffn1_b,
              l0_w_in, l0_b_f, l0_pool_w, l0_pool_scale, l0_w_out,
              l0_ln_mix_g, l0_ln_mix_b,
              l0_ffn2_w1, l0_ffn2_w3, l0_ffn2_w2, l0_ln_ffn2_g, l0_ln_ffn2_b,
              l1_ffn1_w1, l1_ffn1_w3, l1_ffn1_w2, l1_ln_ffn1_g, l1_ln_ffn1_b,
              l1_w_in, l1_conv_w, l1_conv_b, l1_conv_ln_g, l1_conv_ln_b,
              l1_q_norm_g, l1_w_uq, l1_kv_norm_g, l1_w_ukv, l1_w_out,
              l1_ln_mix_g, l1_ln_mix_b,
              l1_ffn2_w1, l1_ffn2_w3, l1_ffn2_w2, l1_ln_ffn2_g, l1_ln_ffn2_b):
    layers = [
        dict(ffn1=(l0_ffn1_w1, l0_ffn1_w3, l0_ffn1_w2), ln_ffn1=(l0_ln_ffn1_g, l0_ln_ffn1_b),
             mixer=(l0_w_in, l0_b_f, l0_pool_w, l0_pool_scale, l0_w_out),
             ln_mix=(l0_ln_mix_g, l0_ln_mix_b),
             ffn2=(l0_ffn2_w1, l0_ffn2_w3, l0_ffn2_w2), ln_ffn2=(l0_ln_ffn2_g, l0_ln_ffn2_b)),
        dict(ffn1=(l1_ffn1_w1, l1_ffn1_w3, l1_ffn1_w2), ln_ffn1=(l1_ln_ffn1_g, l1_ln_ffn1_b),
             mixer=(l1_w_in, l1_conv_w, l1_conv_b, l1_conv_ln_g, l1_conv_ln_b,
                    l1_q_norm_g, l1_w_uq, l1_kv_norm_g, l1_w_ukv, l1_w_out),
             ln_mix=(l1_ln_mix_g, l1_ln_mix_b),
             ffn2=(l1_ffn2_w1, l1_ffn2_w3, l1_ffn2_w2), ln_ffn2=(l1_ln_ffn2_g, l1_ln_ffn2_b)),
    ]
    for i in range(DEPTH):
        p = layers[i]
        x = layer_norm(ALPHA * x + 0.5 * swiglu(x, *p['ffn1']), *p['ln_ffn1'])
        if i % 2 == 0:
            mix = mixer_fox_pool(x, *p['mixer'])
        else:
            mix = mixer_conv_mla(x, positions, *p['mixer'])
        x = layer_norm(ALPHA * x + mix, *p['ln_mix'])
        x = layer_norm(ALPHA * x + 0.5 * swiglu(x, *p['ffn2']), *p['ln_ffn2'])
    return x
```

```python
import functools
import math

import numpy as np
import jax
import jax.numpy as jnp
from jax import lax
from jax.experimental import pallas as pl
from jax.experimental.pallas import tpu as pltpu

F32 = jnp.float32
BF16 = jnp.bfloat16

DEPTH = 2
ALPHA = (2 * DEPTH) ** 0.25
LN_EPS = 1e-5
RMS_EPS = 1e-6
HEAD_DIM = 128
MLA_ROPE_DIM = 64
ROPE_BASE = 10000.0
POOL_WINDOWS = (2, 4, 8, 16)
CONV_TAPS = 31

V7X_VMEM_LIMIT_BYTES = 56 * 1024 * 1024
LANES = 128
SUBLANES = 8
POOL_HALO = 16
CONV_HALO = 32
GATE_LANE_STRIDE = 32


def _pick(n, pref, align):
    best = None
    for d in range(align, min(n, pref) + 1, align):
        if n % d == 0:
            best = d
    return n if best is None else best


def _cparams(*sem):
    return pltpu.CompilerParams(dimension_semantics=sem, vmem_limit_bytes=V7X_VMEM_LIMIT_BYTES)


def _split3(x):
    hi = x.astype(BF16)
    r1 = x - hi.astype(F32)
    mid = r1.astype(BF16)
    lo = (r1 - mid.astype(F32)).astype(BF16)
    return hi, mid, lo


def _gateup_body(x_ref, w1_ref, w3_ref, o_ref):
    x = x_ref[...]
    a = jnp.dot(x, w1_ref[...], preferred_element_type=F32)
    b = jnp.dot(x, w3_ref[...], preferred_element_type=F32)
    o_ref[...] = (a * jax.nn.sigmoid(a) * b).astype(o_ref.dtype)


def _ffn_gateup(xb, w1, w3):
    s, d = xb.shape
    fp = w1.shape[1]
    tm = _pick(s, 1024, SUBLANES)
    tn = _pick(fp, 512, LANES)
    return pl.pallas_call(
        _gateup_body,
        grid=(s // tm, fp // tn),
        in_specs=[pl.BlockSpec((tm, d), lambda i, j: (i, 0)),
                  pl.BlockSpec((d, tn), lambda i, j: (0, j)),
                  pl.BlockSpec((d, tn), lambda i, j: (0, j))],
        out_specs=pl.BlockSpec((tm, tn), lambda i, j: (i, j)),
        out_shape=jax.ShapeDtypeStruct((s, fp), BF16),
        compiler_params=_cparams("parallel", "arbitrary"),
        name="ffn_gateup",
    )(xb, w1, w3)


def _proj_ln_body(y_ref, w_ref, x_ref, g_ref, b_ref, o32_ref, *rest, r, nk, rows, ncols, emit_bf16):
    o16_ref = rest[0] if emit_bf16 else None
    k = pl.program_id(1)
    y = y_ref[...]
    d = o32_ref.shape[1]
    for c0 in range(0, d, ncols):
        part = jnp.dot(y, w_ref[:, pl.ds(c0, ncols)], preferred_element_type=F32)

        @pl.when(k == 0)
        def _():
            o32_ref[:, pl.ds(c0, ncols)] = part

        @pl.when(k > 0)
        def _():
            o32_ref[:, pl.ds(c0, ncols)] += part

    @pl.when(k == nk - 1)
    def _():
        g = g_ref[...]
        b = b_ref[...]
        tm = o32_ref.shape[0]

        def chunk(c, carry):
            r0 = pl.multiple_of(c * rows, rows)
            z = ALPHA * x_ref[pl.ds(r0, rows), :] + r * o32_ref[pl.ds(r0, rows), :]
            mu = jnp.mean(z, axis=-1, keepdims=True)
            zc = z - mu
            var = jnp.mean(zc * zc, axis=-1, keepdims=True)
            out = zc * lax.rsqrt(var + LN_EPS) * g + b
            o32_ref[pl.ds(r0, rows), :] = out
            if emit_bf16:
                o16_ref[pl.ds(r0, rows), :] = out.astype(BF16)
            return carry

        lax.fori_loop(0, tm // rows, chunk, 0)


def _proj_ln(y, w, x, g, b, r, emit_bf16=True):
    s, kdim = y.shape
    d = w.shape[1]
    tm = _pick(s, 512, SUBLANES)
    tk = _pick(kdim, 512, LANES)
    nk = kdim // tk
    rows = _pick(tm, 16, SUBLANES)
    ncols = _pick(d, 1024, LANES)
    out_shape = [jax.ShapeDtypeStruct((s, d), F32)]
    out_specs = [pl.BlockSpec((tm, d), lambda i, k: (i, 0))]
    if emit_bf16:
        out_shape.append(jax.ShapeDtypeStruct((s, d), BF16))
        out_specs.append(pl.BlockSpec((tm, d), lambda i, k: (i, 0)))
    outs = pl.pallas_call(
        functools.partial(_proj_ln_body, r=r, nk=nk, rows=rows, ncols=ncols, emit_bf16=emit_bf16),
        grid=(s // tm, nk),
        in_specs=[pl.BlockSpec((tm, tk), lambda i, k: (i, k)),
                  pl.BlockSpec((tk, d), lambda i, k: (k, 0)),
                  pl.BlockSpec((tm, d), lambda i, k: (i, 0)),
                  pl.BlockSpec((1, d), lambda i, k: (0, 0)),
                  pl.BlockSpec((1, d), lambda i, k: (0, 0))],
        out_specs=out_specs,
        out_shape=out_shape,
        compiler_params=_cparams("parallel", "arbitrary"),
        name="proj_residual_ln",
    )(y, w, x, g.reshape(1, d), b.reshape(1, d))
    return outs if emit_bf16 else (outs[0], None)


def _proj_body(x_ref, w_ref, o_ref, *, scale, scaled_blocks):
    acc = jnp.dot(x_ref[...], w_ref[...], preferred_element_type=F32)
    if scaled_blocks:
        acc = acc * jnp.where(pl.program_id(1) < scaled_blocks, scale, 1.0).astype(F32)
    o_ref[...] = acc.astype(o_ref.dtype)


def _proj(xb, w, out_dtype, tn_pref, scale=1.0, scaled_cols=0):
    s, d = xb.shape
    n = w.shape[1]
    tm = _pick(s, 1024, SUBLANES)
    tn = _pick(n, tn_pref, LANES)
    if scaled_cols:
        tn = math.gcd(tn, scaled_cols)
    return pl.pallas_call(
        functools.partial(_proj_body, scale=scale, scaled_blocks=scaled_cols // tn),
        grid=(s // tm, n // tn),
        in_specs=[pl.BlockSpec((tm, d), lambda i, j: (i, 0)),
                  pl.BlockSpec((d, tn), lambda i, j: (0, j))],
        out_specs=pl.BlockSpec((tm, tn), lambda i, j: (i, j)),
        out_shape=jax.ShapeDtypeStruct((s, n), out_dtype),
        compiler_params=_cparams("parallel", "arbitrary"),
        name="proj",
    )(xb, w)


def _gates_body(f_ref, b_ref, o_ref, carry_ref, *, heads):
    i = pl.program_id(0)
    tm = f_ref.shape[0]

    @pl.when(i == 0)
    def _():
        carry_ref[...] = jnp.zeros_like(carry_ref)

    lane = lax.broadcasted_iota(jnp.int32, (tm, LANES), 1)
    xv = f_ref[...] + b_ref[...]
    logf = jnp.minimum(xv, 0.0) - jnp.log1p(jnp.exp(-jnp.abs(xv)))
    logf = jnp.where(lane < heads, logf, 0.0)
    row = lax.broadcasted_iota(jnp.int32, (tm, tm), 0)
    col = lax.broadcasted_iota(jnp.int32, (tm, tm), 1)
    tril = jnp.where(row >= col, 1.0, 0.0).astype(BF16)
    hi, mid, lo = _split3(logf)
    c = (jnp.dot(tril, hi, preferred_element_type=F32)
         + jnp.dot(tril, mid, preferred_element_type=F32)
         + jnp.dot(tril, lo, preferred_element_type=F32)) + carry_ref[...]
    carry_ref[...] = c[tm - 1:tm, :]
    nhi, nmid, nlo = _split3(-c)
    packed = jnp.where(lane < GATE_LANE_STRIDE, nhi.astype(F32),
                       jnp.where(lane < 2 * GATE_LANE_STRIDE,
                                 pltpu.roll(nmid.astype(F32), GATE_LANE_STRIDE, 1),
                                 pltpu.roll(nlo.astype(F32), 2 * GATE_LANE_STRIDE, 1)))
    o_ref[...] = packed.astype(BF16)


def _fox_gates(uf, ucols, b_f_pad, heads):
    s = uf.shape[0]
    tm = _pick(s, 512, SUBLANES)
    fblk = ucols // LANES
    return pl.pallas_call(
        functools.partial(_gates_body, heads=heads),
        grid=(s // tm,),
        in_specs=[pl.BlockSpec((tm, LANES), lambda i: (i, fblk)),
                  pl.BlockSpec((1, LANES), lambda i: (0, 0))],
        out_specs=pl.BlockSpec((tm, LANES), lambda i: (i, 0)),
        out_shape=jax.ShapeDtypeStruct((s, LANES), BF16),
        scratch_shapes=[pltpu.VMEM((1, LANES), F32)],
        compiler_params=_cparams("arbitrary"),
        name="fox_gates",
    )(uf, b_f_pad)


def _flash_body(*refs, tq, tk, fox):
    if fox:
        q1_ref, k1_ref, k2_ref, v_ref, o_ref, m_sc, l_sc, acc_sc = refs
    else:
        q1_ref, q2_ref, k1_ref, k2_ref, v_ref, o_ref, m_sc, l_sc, acc_sc = refs
    h = pl.program_id(0)
    qi = pl.program_id(1)
    q1 = q1_ref[...]
    if fox:
        lane = lax.broadcasted_iota(jnp.int32, (tq, LANES), 1)
        sel = (lane < 3 * GATE_LANE_STRIDE) & (jnp.bitwise_and(lane, GATE_LANE_STRIDE - 1) == h)
        q2 = jnp.where(sel, 1.0, 0.0).astype(BF16)
    else:
        q2 = q2_ref[...]
    q = jnp.concatenate([q1, q2], axis=1)

    m_sc[...] = jnp.full(m_sc.shape, -jnp.inf, F32)
    l_sc[...] = jnp.zeros(l_sc.shape, F32)
    acc_sc[...] = jnp.zeros(acc_sc.shape, F32)

    def step(j, masked):
        k0 = pl.multiple_of(j * tk, tk)
        kk = jnp.concatenate([k1_ref[pl.ds(k0, tk), :], k2_ref[pl.ds(k0, tk), :]], axis=1)
        s = lax.dot_general(q, kk, (((1,), (1,)), ((), ())), preferred_element_type=F32)
        if masked:
            row = qi * tq + lax.broadcasted_iota(jnp.int32, (tq, tk), 0)
            col = j * tk + lax.broadcasted_iota(jnp.int32, (tq, tk), 1)
            s = jnp.where(row >= col, s, -jnp.inf)
        m_prev = m_sc[...]
        m_new = jnp.maximum(m_prev, jnp.max(s, axis=1, keepdims=True))
        alpha = jnp.exp(m_prev - m_new)
        p = jnp.exp(s - m_new)
        l_sc[...] = alpha * l_sc[...] + jnp.sum(p, axis=1, keepdims=True)
        acc_sc[...] = alpha * acc_sc[...] + jnp.dot(p.astype(BF16), v_ref[pl.ds(k0, tk), :],
                                                    preferred_element_type=F32)
        m_sc[...] = m_new

    per_q = tq // tk
    nfull = qi * per_q

    def full_step(j, carry):
        step(j, False)
        return carry

    lax.fori_loop(0, nfull, full_step, 0)
    for dblk in range(per_q):
        step(nfull + dblk, True)
    o_ref[...] = (acc_sc[...] / l_sc[...]).astype(o_ref.dtype)


def _flash(q, q1_blk, q2_blk, kv, k1_blk, v_blk, k2, heads, tq_pref=512, tk_pref=512):
    s = q.shape[0]
    tq = _pick(s, tq_pref, LANES)
    tk = _pick(tq, tk_pref, LANES)
    fox = q2_blk is None
    in_specs = [pl.BlockSpec((tq, HEAD_DIM), lambda h, i: (i, q1_blk + h))]
    args = [q]
    if not fox:
        in_specs.append(pl.BlockSpec((tq, HEAD_DIM), lambda h, i: (i, q2_blk + h)))
        args.append(q)
    in_specs += [pl.BlockSpec((s, HEAD_DIM), lambda h, i: (0, k1_blk + h)),
                 pl.BlockSpec((s, LANES), lambda h, i: (0, 0)),
                 pl.BlockSpec((s, HEAD_DIM), lambda h, i: (0, v_blk + h))]
    args += [kv, k2, kv]
    return pl.pallas_call(
        functools.partial(_flash_body, tq=tq, tk=tk, fox=fox),
        grid=(heads, s // tq),
        in_specs=in_specs,
        out_specs=pl.BlockSpec((tq, HEAD_DIM), lambda h, i: (i, h)),
        out_shape=jax.ShapeDtypeStruct((s, heads * HEAD_DIM), BF16),
        scratch_shapes=[pltpu.VMEM((tq, 1), F32), pltpu.VMEM((tq, 1), F32),
                        pltpu.VMEM((tq, HEAD_DIM), F32)],
        compiler_params=_cparams("parallel", "arbitrary"),
        name="flash_fox" if fox else "flash_mla",
    )(*args)


def _pool_body(u_ref, halo_ref, w_ref, sc_ref, o_ref, buf, *, groups, gdim):
    i = pl.program_id(0)
    tm = u_ref.shape[0]
    buf[pl.ds(0, POOL_HALO), :] = jnp.where(i == 0, 0.0, halo_ref[...])
    buf[pl.ds(POOL_HALO, tm), :] = u_ref[...]
    t = (i * tm + lax.broadcasted_iota(jnp.int32, (tm, 1), 0)).astype(F32)
    for g in range(groups):
        win = POOL_WINDOWS[g]
        c0 = g * gdim
        tot = buf[pl.ds(POOL_HALO, tm), pl.ds(c0, gdim)]
        cur = tot
        for lag in range(1, win):
            tot = tot + buf[pl.ds(POOL_HALO - lag, tm), pl.ds(c0, gdim)]
        count = jnp.minimum(t + 1.0, float(win))
        pooled = tot / count - cur
        mixed = jnp.dot(pooled.astype(BF16), w_ref[g], preferred_element_type=F32)
        o_ref[:, pl.ds(c0, gdim)] = (mixed * sc_ref[:, pl.ds(c0, gdim)]).astype(o_ref.dtype)


def _pool(uf, pool_w_b, pool_scale):
    s = uf.shape[0]
    groups, gdim, _ = pool_w_b.shape
    c = groups * gdim
    tm = _pick(s, 256, POOL_HALO)
    hb = tm // POOL_HALO
    return pl.pallas_call(
        functools.partial(_pool_body, groups=groups, gdim=gdim),
        grid=(s // tm,),
        in_specs=[pl.BlockSpec((tm, c), lambda i: (i, 0)),
                  pl.BlockSpec((POOL_HALO, c), lambda i: (jnp.maximum(i * hb - 1, 0), 0)),
                  pl.BlockSpec((groups, gdim, gdim), lambda i: (0, 0, 0)),
                  pl.BlockSpec((1, c), lambda i: (0, 0))],
        out_specs=pl.BlockSpec((tm, c), lambda i: (i, 0)),
        out_shape=jax.ShapeDtypeStruct((s, c), BF16),
        scratch_shapes=[pltpu.VMEM((tm + POOL_HALO, c), F32)],
        compiler_params=_cparams("parallel"),
        name="multiscale_pool",
    )(uf, uf, pool_w_b, pool_scale.reshape(1, c))


def _conv_body(a_ref, g_ref, ha_ref, hg_ref, w_ref, cb_ref, lg_ref, lb_ref, o_ref, buf, ybuf, *, rows, lanes):
    i = pl.program_id(0)
    tm, c = a_ref.shape
    halo = ha_ref[...] * jax.nn.sigmoid(hg_ref[...])
    buf[pl.ds(0, CONV_HALO), :] = jnp.where(i == 0, 0.0, halo)
    buf[pl.ds(CONV_HALO, tm), :] = a_ref[...] * jax.nn.sigmoid(g_ref[...])
    base = CONV_HALO - (CONV_TAPS - 1)
    for r0 in range(0, tm, rows):
        for c0 in range(0, c, lanes):
            acc = jnp.zeros((rows, lanes), F32) + cb_ref[:, pl.ds(c0, lanes)]
            for j in range(CONV_TAPS):
                acc = acc + w_ref[pl.ds(j, 1), pl.ds(c0, lanes)] * buf[pl.ds(base + r0 + j, rows), pl.ds(c0, lanes)]
            ybuf[pl.ds(r0, rows), pl.ds(c0, lanes)] = acc
    y = ybuf[...]
    mu = jnp.mean(y, axis=-1, keepdims=True)
    yc = y - mu
    var = jnp.mean(yc * yc, axis=-1, keepdims=True)
    z = yc * lax.rsqrt(var + LN_EPS) * lg_ref[...] + lb_ref[...]
    o_ref[...] = (z * jax.nn.sigmoid(z)).astype(o_ref.dtype)


def _conformer_conv(h1, cw, conv_w_p, conv_b, ln_g, ln_b):
    s = h1.shape[0]
    tm = _pick(s, 128, CONV_HALO)
    hb = tm // CONV_HALO
    rows = _pick(tm, 32, SUBLANES)
    lanes = _pick(cw, 256, LANES)
    nblk = 1
    vec = lambda a: a.reshape(1, cw)
    return pl.pallas_call(
        functools.partial(_conv_body, rows=rows, lanes=lanes),
        grid=(s // tm,),
        in_specs=[pl.BlockSpec((tm, cw), lambda i: (i, 0)),
                  pl.BlockSpec((tm, cw), lambda i: (i, nblk)),
                  pl.BlockSpec((CONV_HALO, cw), lambda i: (jnp.maximum(i * hb - 1, 0), 0)),
                  pl.BlockSpec((CONV_HALO, cw), lambda i: (jnp.maximum(i * hb - 1, 0), nblk)),
                  pl.BlockSpec((CONV_HALO, cw), lambda i: (0, 0)),
                  pl.BlockSpec((1, cw), lambda i: (0, 0)),
                  pl.BlockSpec((1, cw), lambda i: (0, 0)),
                  pl.BlockSpec((1, cw), lambda i: (0, 0))],
        out_specs=pl.BlockSpec((tm, cw), lambda i: (i, 0)),
        out_shape=jax.ShapeDtypeStruct((s, cw), BF16),
        scratch_shapes=[pltpu.VMEM((tm + CONV_HALO, cw), F32), pltpu.VMEM((tm, cw), F32)],
        compiler_params=_cparams("parallel"),
        name="conformer_conv",
    )(h1, h1, h1, h1, conv_w_p, vec(conv_b), vec(ln_g), vec(ln_b))


def _rope128(r, cosv, sinv, lane):
    half = MLA_ROPE_DIM // 2
    s1 = jnp.where(lane < half, -sinv, 0.0)
    s2 = jnp.where((lane >= half) & (lane < 2 * half), sinv, 0.0)
    return r * cosv + pltpu.roll(r, LANES - half, 1) * s1 + pltpu.roll(r, half, 1) * s2


def _rope_k_body(pos_ref, invf_ref, kpe_ref, cos_ref, sin_ref, k2_ref):
    tm = pos_ref.shape[0]
    ang = pos_ref[...].astype(F32) * invf_ref[...]
    cosv = jnp.cos(ang)
    sinv = jnp.sin(ang)
    cos_ref[...] = cosv
    sin_ref[...] = sinv
    lane = lax.broadcasted_iota(jnp.int32, (tm, LANES), 1)
    k2_ref[...] = _rope128(kpe_ref[...], cosv, sinv, lane).astype(BF16)


def _rope_k(positions_col, invf, h1, kpe_block):
    s = h1.shape[0]
    tm = _pick(s, 512, SUBLANES)
    tab = jax.ShapeDtypeStruct((s, LANES), F32)
    return pl.pallas_call(
        _rope_k_body,
        grid=(s // tm,),
        in_specs=[pl.BlockSpec((tm, 1), lambda i: (i, 0)),
                  pl.BlockSpec((1, LANES), lambda i: (0, 0)),
                  pl.BlockSpec((tm, LANES), lambda i: (i, kpe_block))],
        out_specs=[pl.BlockSpec((tm, LANES), lambda i: (i, 0))] * 3,
        out_shape=[tab, tab, jax.ShapeDtypeStruct((s, LANES), BF16)],
        compiler_params=_cparams("parallel"),
        name="rope_tables_key",
    )(positions_col, invf, h1)


def _rms_proj_body(c_ref, g_ref, w_ref, cos_ref, sin_ref, o_ref, xn_ref, *, scale, rope_from):
    j = pl.program_id(1)
    tm, tn = o_ref.shape

    @pl.when(j == 0)
    def _():
        xf = c_ref[...]
        ms = jnp.mean(xf * xf, axis=-1, keepdims=True)
        xn_ref[...] = (xf * lax.rsqrt(ms + RMS_EPS) * g_ref[...]).astype(BF16)

    acc = jnp.dot(xn_ref[...], w_ref[...], preferred_element_type=F32)

    def plain():
        o_ref[...] = (acc * scale).astype(o_ref.dtype)

    if rope_from is None:
        plain()
        return

    pl.when(j < rope_from)(plain)

    @pl.when(j >= rope_from)
    def _():
        lane = lax.broadcasted_iota(jnp.int32, (tm, LANES), 1)
        cosv = cos_ref[...]
        sinv = sin_ref[...]
        for c0 in range(0, tn, LANES):
            r = _rope128(acc[:, c0:c0 + LANES], cosv, sinv, lane)
            o_ref[:, pl.ds(c0, LANES)] = (r * scale).astype(o_ref.dtype)


def _rms_proj(h1, cblock, rank, g, w, cosv, sinv, scale, rope_cols):
    s = h1.shape[0]
    n = w.shape[1]
    tm = _pick(s, 1024, SUBLANES)
    tn = _pick(n, 512, LANES)
    if rope_cols:
        tn = math.gcd(tn, n - rope_cols)
    rope_from = (n - rope_cols) // tn if rope_cols else None
    return pl.pallas_call(
        functools.partial(_rms_proj_body, scale=scale, rope_from=rope_from),
        grid=(s // tm, n // tn),
        in_specs=[pl.BlockSpec((tm, rank), lambda i, j: (i, cblock)),
                  pl.BlockSpec((1, rank), lambda i, j: (0, 0)),
                  pl.BlockSpec((rank, tn), lambda i, j: (0, j)),
                  pl.BlockSpec((tm, LANES), lambda i, j: (i, 0)),
                  pl.BlockSpec((tm, LANES), lambda i, j: (i, 0))],
        out_specs=pl.BlockSpec((tm, tn), lambda i, j: (i, j)),
        out_shape=jax.ShapeDtypeStruct((s, n), BF16),
        scratch_shapes=[pltpu.VMEM((tm, rank), BF16)],
        compiler_params=_cparams("parallel", "arbitrary"),
        name="rms_up_proj",
    )(h1, g.reshape(1, rank), w, cosv, sinv)


def _ffn(x32, x16, w1, w3, w2, g, b, emit_bf16=True):
    d, f = w1.shape
    fp = -(-f // 512) * 512
    w1p = jnp.pad(w1.astype(BF16), ((0, 0), (0, fp - f)))
    w3p = jnp.pad(w3.astype(BF16), ((0, 0), (0, fp - f)))
    w2p = jnp.pad(w2.astype(BF16), ((0, fp - f), (0, 0)))
    hidden = _ffn_gateup(x16, w1p, w3p)
    return _proj_ln(hidden, w2p, x32, g, b, 0.5, emit_bf16)


def _mixer_fox_pool(x32, x16, w_in, b_f, pool_w, pool_scale, w_out, g, b):
    d = x32.shape[1]
    heads = b_f.shape[0]
    fw = heads * HEAD_DIM
    groups, gdim, _ = pool_w.shape
    pw = groups * gdim
    w_qkv = w_in[:, :3 * fw].astype(BF16)
    w_uf = jnp.concatenate([w_in[:, 3 * fw + heads:], w_in[:, 3 * fw:3 * fw + heads],
                            jnp.zeros((d, LANES - heads), F32)], axis=1).astype(BF16)
    b_f_pad = jnp.pad(b_f, (0, LANES - heads)).reshape(1, LANES)

    qkv = _proj(x16, w_qkv, BF16, 1024, scale=HEAD_DIM ** -0.5, scaled_cols=fw)
    uf = _proj(x16, w_uf, F32, pw + LANES)
    gate_parts = _fox_gates(uf, pw, b_f_pad, heads)
    y_a = _flash(qkv, 0, None, qkv, heads, 2 * heads, gate_parts, heads)
    y_b = _pool(uf, pool_w.astype(BF16), pool_scale)
    y = jnp.concatenate([y_a, y_b], axis=1)
    return _proj_ln(y, w_out.astype(BF16), x32, g, b, 1.0)


def _mixer_conv_mla(x32, x16, positions, w_in, conv_w, conv_b, conv_ln_g, conv_ln_b,
                    q_norm_g, w_uq, kv_norm_g, w_ukv, w_out, g, b):
    s, d = x32.shape
    cw = conv_b.shape[0]
    q_rank = q_norm_g.shape[0]
    kv_rank = kv_norm_g.shape[0]
    qk_dim = HEAD_DIM + MLA_ROPE_DIM
    heads = w_uq.shape[1] // qk_dim
    hw = heads * HEAD_DIM
    n_in = w_in.shape[1]
    n_in_p = -(-(n_in + MLA_ROPE_DIM) // 768) * 768
    w_in_p = jnp.pad(w_in.astype(BF16), ((0, 0), (0, n_in_p - n_in)))
    c_q_off = 2 * cw
    c_kv_off = c_q_off + q_rank
    kpe_off = c_kv_off + kv_rank
    assert c_q_off % q_rank == 0 and c_kv_off % kv_rank == 0 and kpe_off % LANES == 0

    wq = w_uq.reshape(q_rank, heads, qk_dim)
    wq_rope = jnp.pad(wq[:, :, HEAD_DIM:], ((0, 0), (0, 0), (0, LANES - MLA_ROPE_DIM)))
    wq_p = jnp.concatenate([wq[:, :, :HEAD_DIM].reshape(q_rank, hw),
                            wq_rope.reshape(q_rank, heads * LANES)], axis=1).astype(BF16)
    wkv = w_ukv.reshape(kv_rank, heads, 2 * HEAD_DIM)
    wkv_p = jnp.concatenate([wkv[:, :, :HEAD_DIM].reshape(kv_rank, hw),
                             wkv[:, :, HEAD_DIM:].reshape(kv_rank, hw)], axis=1).astype(BF16)
    conv_w_p = jnp.pad(conv_w.reshape(CONV_TAPS, cw), ((0, CONV_HALO - CONV_TAPS), (0, 0)))
    half = MLA_ROPE_DIM // 2
    invf = np.zeros((1, LANES), np.float32)
    freqs = ROPE_BASE ** (-np.arange(half, dtype=np.float32) / half)
    invf[0, :half] = freqs
    invf[0, half:2 * half] = freqs
    invf = jnp.asarray(invf)

    h1 = _proj(x16, w_in_p, F32, 768)
    y_c = _conformer_conv(h1, cw, conv_w_p, conv_b, conv_ln_g, conv_ln_b)
    cosv, sinv, k2 = _rope_k(positions.reshape(s, 1), invf, h1, kpe_off // LANES)
    q = _rms_proj(h1, c_q_off // q_rank, q_rank, q_norm_g, wq_p, cosv, sinv,
                  qk_dim ** -0.5, heads * LANES)
    kv = _rms_proj(h1, c_kv_off // kv_rank, kv_rank, kv_norm_g, wkv_p, cosv, sinv, 1.0, 0)
    y_d = _flash(q, 0, heads, kv, 0, heads, k2, heads)
    y = jnp.concatenate([y_c, y_d], axis=1)
    return _proj_ln(y, w_out.astype(BF16), x32, g, b, 1.0)


def kernel(x, positions, l0_ffn1_w1, l0_ffn1_w3, l0_ffn1_w2, l0_ln_ffn1_g, l0_ln_ffn1_b, l0_w_in, l0_b_f, l0_pool_w, l0_pool_scale, l0_w_out, l0_ln_mix_g, l0_ln_mix_b, l0_ffn2_w1, l0_ffn2_w3, l0_ffn2_w2, l0_ln_ffn2_g, l0_ln_ffn2_b, l1_ffn1_w1, l1_ffn1_w3, l1_ffn1_w2, l1_ln_ffn1_g, l1_ln_ffn1_b, l1_w_in, l1_conv_w, l1_conv_b, l1_conv_ln_g, l1_conv_ln_b, l1_q_norm_g, l1_w_uq, l1_kv_norm_g, l1_w_ukv, l1_w_out, l1_ln_mix_g, l1_ln_mix_b, l1_ffn2_w1, l1_ffn2_w3, l1_ffn2_w2, l1_ln_ffn2_g, l1_ln_ffn2_b):
    bsz, s, d = x.shape
    outs = []
    for bi in range(bsz):
        x32 = x.reshape(s, d) if bsz == 1 else x[bi]
        x16 = x32.astype(BF16)
        pos = positions.reshape(s) if bsz == 1 else positions[bi]
        x32, x16 = _ffn(x32, x16, l0_ffn1_w1, l0_ffn1_w3, l0_ffn1_w2, l0_ln_ffn1_g, l0_ln_ffn1_b)
        x32, x16 = _mixer_fox_pool(x32, x16, l0_w_in, l0_b_f, l0_pool_w, l0_pool_scale, l0_w_out,
                                   l0_ln_mix_g, l0_ln_mix_b)
        x32, x16 = _ffn(x32, x16, l0_ffn2_w1, l0_ffn2_w3, l0_ffn2_w2, l0_ln_ffn2_g, l0_ln_ffn2_b)
        x32, x16 = _ffn(x32, x16, l1_ffn1_w1, l1_ffn1_w3, l1_ffn1_w2, l1_ln_ffn1_g, l1_ln_ffn1_b)
        x32, x16 = _mixer_conv_mla(x32, x16, pos, l1_w_in, l1_conv_w, l1_conv_b, l1_conv_ln_g,
                                   l1_conv_ln_b, l1_q_norm_g, l1_w_uq, l1_kv_norm_g, l1_w_ukv,
                                   l1_w_out, l1_ln_mix_g, l1_ln_mix_b)
        x32, _ = _ffn(x32, x16, l1_ffn2_w1, l1_ffn2_w3, l1_ffn2_w2, l1_ln_ffn2_g, l1_ln_ffn2_b,
                      emit_bf16=False)
        outs.append(x32)
    return outs[0].reshape(1, s, d) if bsz == 1 else jnp.stack(outs, axis=0)
```

```python
import functools
import math

import numpy as np
import jax
import jax.numpy as jnp
from jax import lax
from jax.experimental import pallas as pl
from jax.experimental.pallas import tpu as pltpu

F32 = jnp.float32
BF16 = jnp.bfloat16

DEPTH = 2
ALPHA = (2 * DEPTH) ** 0.25
LN_EPS = 1e-5
RMS_EPS = 1e-6
HEAD_DIM = 128
MLA_ROPE_DIM = 64
ROPE_BASE = 10000.0
POOL_WINDOWS = (2, 4, 8, 16)
CONV_TAPS = 31

V7X_VMEM_LIMIT_BYTES = 56 * 1024 * 1024
LANES = 128
SUBLANES = 8
POOL_HALO = 16
CONV_HALO = 32
GATE_LANE_STRIDE = 32
FLASH_CHAINS = 2
LOG2E = math.log2(math.e)


def _pick(n, pref, align):
    best = None
    for d in range(align, min(n, pref) + 1, align):
        if n % d == 0:
            best = d
    return n if best is None else best


def _cparams(*sem):
    return pltpu.CompilerParams(dimension_semantics=sem, vmem_limit_bytes=V7X_VMEM_LIMIT_BYTES)


def _split3(x):
    hi = x.astype(BF16)
    r1 = x - hi.astype(F32)
    mid = r1.astype(BF16)
    lo = (r1 - mid.astype(F32)).astype(BF16)
    return hi, mid, lo


def _gateup_body(x_ref, w1_ref, w3_ref, o_ref):
    x = x_ref[...]
    a = jnp.dot(x, w1_ref[...], preferred_element_type=F32)
    b = jnp.dot(x, w3_ref[...], preferred_element_type=F32)
    o_ref[...] = (a * jax.nn.sigmoid(a) * b).astype(o_ref.dtype)


def _ffn_gateup(xb, w1, w3):
    s, d = xb.shape
    fp = w1.shape[1]
    tm = _pick(s, 1024, SUBLANES)
    tn = _pick(fp, 512, LANES)
    return pl.pallas_call(
        _gateup_body,
        grid=(s // tm, fp // tn),
        in_specs=[pl.BlockSpec((tm, d), lambda i, j: (i, 0)),
                  pl.BlockSpec((d, tn), lambda i, j: (0, j)),
                  pl.BlockSpec((d, tn), lambda i, j: (0, j))],
        out_specs=pl.BlockSpec((tm, tn), lambda i, j: (i, j)),
        out_shape=jax.ShapeDtypeStruct((s, fp), BF16),
        compiler_params=_cparams("parallel", "arbitrary"),
        name="ffn_gateup",
    )(xb, w1, w3)


def _proj_ln_body(y_ref, w_ref, x_ref, g_ref, b_ref, o32_ref, *rest, r, nk, rows, ncols, emit_bf16):
    o16_ref = rest[0] if emit_bf16 else None
    k = pl.program_id(1)
    @pl.when(k == 0)
    def _():
        o32_ref[...] = jnp.zeros(o32_ref.shape, F32)

    y = y_ref[...]
    d = o32_ref.shape[1]
    for c0 in range(0, d, ncols):
        o32_ref[:, pl.ds(c0, ncols)] += jnp.dot(y, w_ref[:, pl.ds(c0, ncols)],
                                                preferred_element_type=F32)

    @pl.when(k == nk - 1)
    def _():
        g = g_ref[...]
        b = b_ref[...]
        tm = o32_ref.shape[0]

        def chunk(c, carry):
            r0 = pl.multiple_of(c * rows, rows)
            z = ALPHA * x_ref[pl.ds(r0, rows), :] + r * o32_ref[pl.ds(r0, rows), :]
            mu = jnp.mean(z, axis=-1, keepdims=True)
            zc = z - mu
            var = jnp.mean(zc * zc, axis=-1, keepdims=True)
            out = zc * lax.rsqrt(var + LN_EPS) * g + b
            o32_ref[pl.ds(r0, rows), :] = out
            if emit_bf16:
                o16_ref[pl.ds(r0, rows), :] = out.astype(BF16)
            return carry

        lax.fori_loop(0, tm // rows, chunk, 0)


def _proj_ln(y, w, x, g, b, r, emit_bf16=True):
    s, kdim = y.shape
    d = w.shape[1]
    tm = _pick(s, 512, SUBLANES)
    tk = _pick(kdim, 512, LANES)
    nk = kdim // tk
    rows = _pick(tm, 16, SUBLANES)
    ncols = _pick(d, 1024, LANES)
    out_shape = [jax.ShapeDtypeStruct((s, d), F32)]
    out_specs = [pl.BlockSpec((tm, d), lambda i, k: (i, 0))]
    if emit_bf16:
        out_shape.append(jax.ShapeDtypeStruct((s, d), BF16))
        out_specs.append(pl.BlockSpec((tm, d), lambda i, k: (i, 0)))
    outs = pl.pallas_call(
        functools.partial(_proj_ln_body, r=r, nk=nk, rows=rows, ncols=ncols, emit_bf16=emit_bf16),
        grid=(s // tm, nk),
        in_specs=[pl.BlockSpec((tm, tk), lambda i, k: (i, k)),
                  pl.BlockSpec((tk, d), lambda i, k: (k, 0)),
                  pl.BlockSpec((tm, d), lambda i, k: (i, 0)),
                  pl.BlockSpec((1, d), lambda i, k: (0, 0)),
                  pl.BlockSpec((1, d), lambda i, k: (0, 0))],
        out_specs=out_specs,
        out_shape=out_shape,
        compiler_params=_cparams("parallel", "arbitrary"),
        name="proj_residual_ln",
    )(y, w, x, g.reshape(1, d), b.reshape(1, d))
    return outs if emit_bf16 else (outs[0], None)


def _proj_body(x_ref, w_ref, o_ref, *, scale, scaled_blocks):
    acc = jnp.dot(x_ref[...], w_ref[...], preferred_element_type=F32)
    if scaled_blocks:
        acc = acc * jnp.where(pl.program_id(1) < scaled_blocks, scale, 1.0).astype(F32)
    o_ref[...] = acc.astype(o_ref.dtype)


def _proj(xb, w, out_dtype, tn_pref, scale=1.0, scaled_cols=0):
    s, d = xb.shape
    n = w.shape[1]
    tm = _pick(s, 1024, SUBLANES)
    tn = _pick(n, tn_pref, LANES)
    if scaled_cols:
        tn = math.gcd(tn, scaled_cols)
    return pl.pallas_call(
        functools.partial(_proj_body, scale=scale, scaled_blocks=scaled_cols // tn),
        grid=(s // tm, n // tn),
        in_specs=[pl.BlockSpec((tm, d), lambda i, j: (i, 0)),
                  pl.BlockSpec((d, tn), lambda i, j: (0, j))],
        out_specs=pl.BlockSpec((tm, tn), lambda i, j: (i, j)),
        out_shape=jax.ShapeDtypeStruct((s, n), out_dtype),
        compiler_params=_cparams("parallel", "arbitrary"),
        name="proj",
    )(xb, w)


def _gates_body(f_ref, b_ref, o_ref, carry_ref, *, heads):
    i = pl.program_id(0)
    tm = f_ref.shape[0]

    @pl.when(i == 0)
    def _():
        carry_ref[...] = jnp.zeros_like(carry_ref)

    lane = lax.broadcasted_iota(jnp.int32, (tm, LANES), 1)
    xv = f_ref[...] + b_ref[...]
    logf = jnp.minimum(xv, 0.0) - jnp.log1p(jnp.exp(-jnp.abs(xv)))
    logf = jnp.where(lane < heads, logf, 0.0)
    row = lax.broadcasted_iota(jnp.int32, (tm, tm), 0)
    col = lax.broadcasted_iota(jnp.int32, (tm, tm), 1)
    tril = jnp.where(row >= col, 1.0, 0.0).astype(BF16)
    hi, mid, lo = _split3(logf)
    c = (jnp.dot(tril, hi, preferred_element_type=F32)
         + jnp.dot(tril, mid, preferred_element_type=F32)
         + jnp.dot(tril, lo, preferred_element_type=F32)) + carry_ref[...]
    carry_ref[...] = c[tm - 1:tm, :]
    nhi, nmid, nlo = _split3(-LOG2E * c)
    packed = jnp.where(lane < GATE_LANE_STRIDE, nhi.astype(F32),
                       jnp.where(lane < 2 * GATE_LANE_STRIDE,
                                 pltpu.roll(nmid.astype(F32), GATE_LANE_STRIDE, 1),
                                 pltpu.roll(nlo.astype(F32), 2 * GATE_LANE_STRIDE, 1)))
    o_ref[...] = packed.astype(BF16)


def _fox_gates(uf, ucols, b_f_pad, heads):
    s = uf.shape[0]
    tm = _pick(s, 512, SUBLANES)
    fblk = ucols // LANES
    return pl.pallas_call(
        functools.partial(_gates_body, heads=heads),
        grid=(s // tm,),
        in_specs=[pl.BlockSpec((tm, LANES), lambda i: (i, fblk)),
                  pl.BlockSpec((1, LANES), lambda i: (0, 0))],
        out_specs=pl.BlockSpec((tm, LANES), lambda i: (i, 0)),
        out_shape=jax.ShapeDtypeStruct((s, LANES), BF16),
        scratch_shapes=[pltpu.VMEM((1, LANES), F32)],
        compiler_params=_cparams("arbitrary"),
        name="fox_gates",
    )(uf, b_f_pad)


def _flash_body(*refs, tq, tk, fox, nchain):
    if fox:
        q1_ref, k1_ref, k2_ref, v_ref, o_ref, m_sc, acc_sc = refs
    else:
        q1_ref, q2_ref, k1_ref, k2_ref, v_ref, o_ref, m_sc, acc_sc = refs
    h = pl.program_id(0)
    qi = pl.program_id(1)
    rows = tq // nchain
    if fox:
        lane = lax.broadcasted_iota(jnp.int32, (rows, LANES), 1)
        sel = (lane < 3 * GATE_LANE_STRIDE) & (jnp.bitwise_and(lane, GATE_LANE_STRIDE - 1) == h)
        gate_sel = jnp.where(sel, 1.0, 0.0).astype(BF16)
    qs = []
    for c in range(nchain):
        q1 = q1_ref[pl.ds(c * rows, rows), :]
        q2 = gate_sel if fox else q2_ref[pl.ds(c * rows, rows), :]
        qs.append(jnp.concatenate([q1, q2], axis=1))

    m_sc[...] = jnp.full(m_sc.shape, -jnp.inf, F32)
    acc_sc[...] = jnp.zeros(acc_sc.shape, F32)
    ones = jnp.ones((tk, LANES), BF16)

    def step(j, diag):
        k0 = pl.multiple_of(j * tk, tk)
        kk = jnp.concatenate([k1_ref[pl.ds(k0, tk), :], k2_ref[pl.ds(k0, tk), :]], axis=1)
        vv = jnp.concatenate([v_ref[pl.ds(k0, tk), :], ones], axis=1)
        for c in range(nchain):
            r0 = c * rows
            if diag is not None and diag * tk >= r0 + rows:
                continue
            s = lax.dot_general(qs[c], kk, (((1,), (1,)), ((), ())), preferred_element_type=F32)
            if diag is not None and (diag + 1) * tk - 1 > r0:
                row = r0 + lax.broadcasted_iota(jnp.int32, (rows, tk), 0)
                col = diag * tk + lax.broadcasted_iota(jnp.int32, (rows, tk), 1)
                s = jnp.where(row >= col, s, -jnp.inf)
            m_prev = m_sc[pl.ds(r0, rows), :]
            m_new = jnp.maximum(m_prev, jnp.max(s, axis=1, keepdims=True))
            alpha = jnp.exp2(m_prev - m_new)
            p = jnp.exp2(s - m_new).astype(BF16)
            acc_sc[pl.ds(r0, rows), :] = (alpha * acc_sc[pl.ds(r0, rows), :]
                                          + jnp.dot(p, vv, preferred_element_type=F32))
            m_sc[pl.ds(r0, rows), :] = m_new

    per_q = tq // tk
    nfull = qi * per_q

    def full_step(j, carry):
        step(j, None)
        return carry

    lax.fori_loop(0, nfull, full_step, 0)
    for dblk in range(per_q):
        step(nfull + dblk, dblk)
    o_ref[...] = (acc_sc[:, pl.ds(0, HEAD_DIM)] / acc_sc[:, pl.ds(HEAD_DIM, LANES)]).astype(o_ref.dtype)


def _flash(q, q1_blk, q2_blk, kv, k1_blk, v_blk, k2, heads, tq_pref=1024, tk_pref=1024, nchain=FLASH_CHAINS):
    s = q.shape[0]
    tq = _pick(s, tq_pref, LANES)
    tk = _pick(tq, tk_pref, LANES)
    fox = q2_blk is None
    in_specs = [pl.BlockSpec((tq, HEAD_DIM), lambda h, i: (i, q1_blk + h))]
    args = [q]
    if not fox:
        in_specs.append(pl.BlockSpec((tq, HEAD_DIM), lambda h, i: (i, q2_blk + h)))
        args.append(q)
    in_specs += [pl.BlockSpec((s, HEAD_DIM), lambda h, i: (0, k1_blk + h)),
                 pl.BlockSpec((s, LANES), lambda h, i: (0, 0)),
                 pl.BlockSpec((s, HEAD_DIM), lambda h, i: (0, v_blk + h))]
    args += [kv, k2, kv]
    return pl.pallas_call(
        functools.partial(_flash_body, tq=tq, tk=tk, fox=fox, nchain=nchain),
        grid=(heads, s // tq),
        in_specs=in_specs,
        out_specs=pl.BlockSpec((tq, HEAD_DIM), lambda h, i: (i, h)),
        out_shape=jax.ShapeDtypeStruct((s, heads * HEAD_DIM), BF16),
        scratch_shapes=[pltpu.VMEM((tq, 1), F32), pltpu.VMEM((tq, HEAD_DIM + LANES), F32)],
        compiler_params=_cparams("parallel", "arbitrary"),
        name="flash_fox" if fox else "flash_mla",
    )(*args)


def _pool_body(u_ref, halo_ref, w_ref, sc_ref, o_ref, buf, *, groups, gdim):
    i = pl.program_id(0)
    tm = u_ref.shape[0]
    buf[pl.ds(0, POOL_HALO), :] = jnp.where(i == 0, 0.0, halo_ref[...])
    buf[pl.ds(POOL_HALO, tm), :] = u_ref[...]
    t = (i * tm + lax.broadcasted_iota(jnp.int32, (tm, 1), 0)).astype(F32)
    for g in range(groups):
        win = POOL_WINDOWS[g]
        c0 = g * gdim
        tot = buf[pl.ds(POOL_HALO, tm), pl.ds(c0, gdim)]
        cur = tot
        for lag in range(1, win):
            tot = tot + buf[pl.ds(POOL_HALO - lag, tm), pl.ds(c0, gdim)]
        count = jnp.minimum(t + 1.0, float(win))
        pooled = tot / count - cur
        mixed = jnp.dot(pooled.astype(BF16), w_ref[g], preferred_element_type=F32)
        o_ref[:, pl.ds(c0, gdim)] = (mixed * sc_ref[:, pl.ds(c0, gdim)]).astype(o_ref.dtype)


def _pool(uf, pool_w_b, pool_scale):
    s = uf.shape[0]
    groups, gdim, _ = pool_w_b.shape
    c = groups * gdim
    tm = _pick(s, 256, POOL_HALO)
    hb = tm // POOL_HALO
    return pl.pallas_call(
        functools.partial(_pool_body, groups=groups, gdim=gdim),
        grid=(s // tm,),
        in_specs=[pl.BlockSpec((tm, c), lambda i: (i, 0)),
                  pl.BlockSpec((POOL_HALO, c), lambda i: (jnp.maximum(i * hb - 1, 0), 0)),
                  pl.BlockSpec((groups, gdim, gdim), lambda i: (0, 0, 0)),
                  pl.BlockSpec((1, c), lambda i: (0, 0))],
        out_specs=pl.BlockSpec((tm, c), lambda i: (i, 0)),
        out_shape=jax.ShapeDtypeStruct((s, c), BF16),
        scratch_shapes=[pltpu.VMEM((tm + POOL_HALO, c), F32)],
        compiler_params=_cparams("parallel"),
        name="multiscale_pool",
    )(uf, uf, pool_w_b, pool_scale.reshape(1, c))


def _conv_body(a_ref, g_ref, ha_ref, hg_ref, w_ref, cb_ref, lg_ref, lb_ref, o_ref, buf, ybuf, *, rows, lanes):
    i = pl.program_id(0)
    tm, c = a_ref.shape
    halo = ha_ref[...] * jax.nn.sigmoid(hg_ref[...])
    buf[pl.ds(0, CONV_HALO), :] = jnp.where(i == 0, 0.0, halo)
    buf[pl.ds(CONV_HALO, tm), :] = a_ref[...] * jax.nn.sigmoid(g_ref[...])
    base = CONV_HALO - (CONV_TAPS - 1)
    for r0 in range(0, tm, rows):
        for c0 in range(0, c, lanes):
            acc = jnp.zeros((rows, lanes), F32) + cb_ref[:, pl.ds(c0, lanes)]
            for j in range(CONV_TAPS):
                acc = acc + w_ref[pl.ds(j, 1), pl.ds(c0, lanes)] * buf[pl.ds(base + r0 + j, rows), pl.ds(c0, lanes)]
            ybuf[pl.ds(r0, rows), pl.ds(c0, lanes)] = acc
    y = ybuf[...]
    mu = jnp.mean(y, axis=-1, keepdims=True)
    yc = y - mu
    var = jnp.mean(yc * yc, axis=-1, keepdims=True)
    z = yc * lax.rsqrt(var + LN_EPS) * lg_ref[...] + lb_ref[...]
    o_ref[...] = (z * jax.nn.sigmoid(z)).astype(o_ref.dtype)


def _conformer_conv(h1, cw, conv_w_p, conv_b, ln_g, ln_b):
    s = h1.shape[0]
    tm = _pick(s, 128, CONV_HALO)
    hb = tm // CONV_HALO
    rows = _pick(tm, 32, SUBLANES)
    lanes = _pick(cw, 256, LANES)
    nblk = 1
    vec = lambda a: a.reshape(1, cw)
    return pl.pallas_call(
        functools.partial(_conv_body, rows=rows, lanes=lanes),
        grid=(s // tm,),
        in_specs=[pl.BlockSpec((tm, cw), lambda i: (i, 0)),
                  pl.BlockSpec((tm, cw), lambda i: (i, nblk)),
                  pl.BlockSpec((CONV_HALO, cw), lambda i: (jnp.maximum(i * hb - 1, 0), 0)),
                  pl.BlockSpec((CONV_HALO, cw), lambda i: (jnp.maximum(i * hb - 1, 0), nblk)),
                  pl.BlockSpec((CONV_HALO, cw), lambda i: (0, 0)),
                  pl.BlockSpec((1, cw), lambda i: (0, 0)),
                  pl.BlockSpec((1, cw), lambda i: (0, 0)),
                  pl.BlockSpec((1, cw), lambda i: (0, 0))],
        out_specs=pl.BlockSpec((tm, cw), lambda i: (i, 0)),
        out_shape=jax.ShapeDtypeStruct((s, cw), BF16),
        scratch_shapes=[pltpu.VMEM((tm + CONV_HALO, cw), F32), pltpu.VMEM((tm, cw), F32)],
        compiler_params=_cparams("parallel"),
        name="conformer_conv",
    )(h1, h1, h1, h1, conv_w_p, vec(conv_b), vec(ln_g), vec(ln_b))


def _rope128(r, cosv, sinv, lane):
    half = MLA_ROPE_DIM // 2
    s1 = jnp.where(lane < half, -sinv, 0.0)
    s2 = jnp.where((lane >= half) & (lane < 2 * half), sinv, 0.0)
    return r * cosv + pltpu.roll(r, LANES - half, 1) * s1 + pltpu.roll(r, half, 1) * s2


def _rope_k_body(pos_ref, invf_ref, kpe_ref, cos_ref, sin_ref, k2_ref):
    tm = pos_ref.shape[0]
    ang = pos_ref[...].astype(F32) * invf_ref[...]
    cosv = jnp.cos(ang)
    sinv = jnp.sin(ang)
    cos_ref[...] = cosv
    sin_ref[...] = sinv
    lane = lax.broadcasted_iota(jnp.int32, (tm, LANES), 1)
    k2_ref[...] = _rope128(kpe_ref[...], cosv, sinv, lane).astype(BF16)


def _rope_k(positions_col, invf, h1, kpe_block):
    s = h1.shape[0]
    tm = _pick(s, 512, SUBLANES)
    tab = jax.ShapeDtypeStruct((s, LANES), F32)
    return pl.pallas_call(
        _rope_k_body,
        grid=(s // tm,),
        in_specs=[pl.BlockSpec((tm, 1), lambda i: (i, 0)),
                  pl.BlockSpec((1, LANES), lambda i: (0, 0)),
                  pl.BlockSpec((tm, LANES), lambda i: (i, kpe_block))],
        out_specs=[pl.BlockSpec((tm, LANES), lambda i: (i, 0))] * 3,
        out_shape=[tab, tab, jax.ShapeDtypeStruct((s, LANES), BF16)],
        compiler_params=_cparams("parallel"),
        name="rope_tables_key",
    )(positions_col, invf, h1)


def _rms_proj_body(c_ref, g_ref, w_ref, cos_ref, sin_ref, o_ref, xn_ref, *, scale, rope_from):
    j = pl.program_id(1)
    tm, tn = o_ref.shape

    @pl.when(j == 0)
    def _():
        xf = c_ref[...]
        ms = jnp.mean(xf * xf, axis=-1, keepdims=True)
        xn_ref[...] = (xf * lax.rsqrt(ms + RMS_EPS) * g_ref[...]).astype(BF16)

    acc = jnp.dot(xn_ref[...], w_ref[...], preferred_element_type=F32)

    def plain():
        o_ref[...] = (acc * scale).astype(o_ref.dtype)

    if rope_from is None:
        plain()
        return

    pl.when(j < rope_from)(plain)

    @pl.when(j >= rope_from)
    def _():
        lane = lax.broadcasted_iota(jnp.int32, (tm, LANES), 1)
        cosv = cos_ref[...]
        sinv = sin_ref[...]
        for c0 in range(0, tn, LANES):
            r = _rope128(acc[:, c0:c0 + LANES], cosv, sinv, lane)
            o_ref[:, pl.ds(c0, LANES)] = (r * scale).astype(o_ref.dtype)


def _rms_proj(h1, cblock, rank, g, w, cosv, sinv, scale, rope_cols):
    s = h1.shape[0]
    n = w.shape[1]
    tm = _pick(s, 1024, SUBLANES)
    tn = _pick(n, 512, LANES)
    if rope_cols:
        tn = math.gcd(tn, n - rope_cols)
    rope_from = (n - rope_cols) // tn if rope_cols else None
    return pl.pallas_call(
        functools.partial(_rms_proj_body, scale=scale, rope_from=rope_from),
        grid=(s // tm, n // tn),
        in_specs=[pl.BlockSpec((tm, rank), lambda i, j: (i, cblock)),
                  pl.BlockSpec((1, rank), lambda i, j: (0, 0)),
                  pl.BlockSpec((rank, tn), lambda i, j: (0, j)),
                  pl.BlockSpec((tm, LANES), lambda i, j: (i, 0)),
                  pl.BlockSpec((tm, LANES), lambda i, j: (i, 0))],
        out_specs=pl.BlockSpec((tm, tn), lambda i, j: (i, j)),
        out_shape=jax.ShapeDtypeStruct((s, n), BF16),
        scratch_shapes=[pltpu.VMEM((tm, rank), BF16)],
        compiler_params=_cparams("parallel", "arbitrary"),
        name="rms_up_proj",
    )(h1, g.reshape(1, rank), w, cosv, sinv)


def _ffn(x32, x16, w1, w3, w2, g, b, emit_bf16=True):
    d, f = w1.shape
    fp = -(-f // 512) * 512
    w1p = jnp.pad(w1, ((0, 0), (0, fp - f))).astype(BF16)
    w3p = jnp.pad(w3, ((0, 0), (0, fp - f))).astype(BF16)
    w2p = jnp.pad(w2, ((0, fp - f), (0, 0))).astype(BF16)
    hidden = _ffn_gateup(x16, w1p, w3p)
    return _proj_ln(hidden, w2p, x32, g, b, 0.5, emit_bf16)


def _mixer_fox_pool(x32, x16, w_in, b_f, pool_w, pool_scale, w_out, g, b):
    d = x32.shape[1]
    heads = b_f.shape[0]
    fw = heads * HEAD_DIM
    groups, gdim, _ = pool_w.shape
    pw = groups * gdim
    w_qkv = w_in[:, :3 * fw].astype(BF16)
    w_uf = jnp.concatenate([w_in[:, 3 * fw + heads:], w_in[:, 3 * fw:3 * fw + heads],
                            jnp.zeros((d, LANES - heads), F32)], axis=1).astype(BF16)
    b_f_pad = jnp.pad(b_f, (0, LANES - heads)).reshape(1, LANES)

    qkv = _proj(x16, w_qkv, BF16, 1024, scale=LOG2E * HEAD_DIM ** -0.5, scaled_cols=fw)
    uf = _proj(x16, w_uf, F32, pw + LANES)
    gate_parts = _fox_gates(uf, pw, b_f_pad, heads)
    y_a = _flash(qkv, 0, None, qkv, heads, 2 * heads, gate_parts, heads)
    y_b = _pool(uf, pool_w.astype(BF16), pool_scale)
    y = jnp.concatenate([y_a, y_b], axis=1)
    return _proj_ln(y, w_out.astype(BF16), x32, g, b, 1.0)


def _mixer_conv_mla(x32, x16, positions, w_in, conv_w, conv_b, conv_ln_g, conv_ln_b,
                    q_norm_g, w_uq, kv_norm_g, w_ukv, w_out, g, b):
    s, d = x32.shape
    cw = conv_b.shape[0]
    q_rank = q_norm_g.shape[0]
    kv_rank = kv_norm_g.shape[0]
    qk_dim = HEAD_DIM + MLA_ROPE_DIM
    heads = w_uq.shape[1] // qk_dim
    hw = heads * HEAD_DIM
    n_in = w_in.shape[1]
    n_in_p = -(-(n_in + MLA_ROPE_DIM) // 768) * 768
    w_in_p = jnp.pad(w_in.astype(BF16), ((0, 0), (0, n_in_p - n_in)))
    c_q_off = 2 * cw
    c_kv_off = c_q_off + q_rank
    kpe_off = c_kv_off + kv_rank
    assert c_q_off % q_rank == 0 and c_kv_off % kv_rank == 0 and kpe_off % LANES == 0

    wq = w_uq.reshape(q_rank, heads, qk_dim)
    wq_rope = jnp.pad(wq[:, :, HEAD_DIM:], ((0, 0), (0, 0), (0, LANES - MLA_ROPE_DIM)))
    wq_p = jnp.concatenate([wq[:, :, :HEAD_DIM].reshape(q_rank, hw),
                            wq_rope.reshape(q_rank, heads * LANES)], axis=1).astype(BF16)
    wkv = w_ukv.reshape(kv_rank, heads, 2 * HEAD_DIM)
    wkv_p = jnp.concatenate([wkv[:, :, :HEAD_DIM].reshape(kv_rank, hw),
                             wkv[:, :, HEAD_DIM:].reshape(kv_rank, hw)], axis=1).astype(BF16)
    conv_w_p = jnp.pad(conv_w.reshape(CONV_TAPS, cw), ((0, CONV_HALO - CONV_TAPS), (0, 0)))
    half = MLA_ROPE_DIM // 2
    invf = np.zeros((1, LANES), np.float32)
    freqs = ROPE_BASE ** (-np.arange(half, dtype=np.float32) / half)
    invf[0, :half] = freqs
    invf[0, half:2 * half] = freqs
    invf = jnp.asarray(invf)

    h1 = _proj(x16, w_in_p, F32, 768)
    y_c = _conformer_conv(h1, cw, conv_w_p, conv_b, conv_ln_g, conv_ln_b)
    cosv, sinv, k2 = _rope_k(positions.reshape(s, 1), invf, h1, kpe_off // LANES)
    q = _rms_proj(h1, c_q_off // q_rank, q_rank, q_norm_g, wq_p, cosv, sinv,
                  LOG2E * qk_dim ** -0.5, heads * LANES)
    kv = _rms_proj(h1, c_kv_off // kv_rank, kv_rank, kv_norm_g, wkv_p, cosv, sinv, 1.0, 0)
    y_d = _flash(q, 0, heads, kv, 0, heads, k2, heads)
    y = jnp.concatenate([y_c, y_d], axis=1)
    return _proj_ln(y, w_out.astype(BF16), x32, g, b, 1.0)


def kernel(x, positions, l0_ffn1_w1, l0_ffn1_w3, l0_ffn1_w2, l0_ln_ffn1_g, l0_ln_ffn1_b, l0_w_in, l0_b_f, l0_pool_w, l0_pool_scale, l0_w_out, l0_ln_mix_g, l0_ln_mix_b, l0_ffn2_w1, l0_ffn2_w3, l0_ffn2_w2, l0_ln_ffn2_g, l0_ln_ffn2_b, l1_ffn1_w1, l1_ffn1_w3, l1_ffn1_w2, l1_ln_ffn1_g, l1_ln_ffn1_b, l1_w_in, l1_conv_w, l1_conv_b, l1_conv_ln_g, l1_conv_ln_b, l1_q_norm_g, l1_w_uq, l1_kv_norm_g, l1_w_ukv, l1_w_out, l1_ln_mix_g, l1_ln_mix_b, l1_ffn2_w1, l1_ffn2_w3, l1_ffn2_w2, l1_ln_ffn2_g, l1_ln_ffn2_b):
    bsz, s, d = x.shape
    outs = []
    for bi in range(bsz):
        x32 = x.reshape(s, d) if bsz == 1 else x[bi]
        x16 = x32.astype(BF16)
        pos = positions.reshape(s) if bsz == 1 else positions[bi]
        x32, x16 = _ffn(x32, x16, l0_ffn1_w1, l0_ffn1_w3, l0_ffn1_w2, l0_ln_ffn1_g, l0_ln_ffn1_b)
        x32, x16 = _mixer_fox_pool(x32, x16, l0_w_in, l0_b_f, l0_pool_w, l0_pool_scale, l0_w_out,
                                   l0_ln_mix_g, l0_ln_mix_b)
        x32, x16 = _ffn(x32, x16, l0_ffn2_w1, l0_ffn2_w3, l0_ffn2_w2, l0_ln_ffn2_g, l0_ln_ffn2_b)
        x32, x16 = _ffn(x32, x16, l1_ffn1_w1, l1_ffn1_w3, l1_ffn1_w2, l1_ln_ffn1_g, l1_ln_ffn1_b)
        x32, x16 = _mixer_conv_mla(x32, x16, pos, l1_w_in, l1_conv_w, l1_conv_b, l1_conv_ln_g,
                                   l1_conv_ln_b, l1_q_norm_g, l1_w_uq, l1_kv_norm_g, l1_w_ukv,
                                   l1_w_out, l1_ln_mix_g, l1_ln_mix_b)
        x32, _ = _ffn(x32, x16, l1_ffn2_w1, l1_ffn2_w3, l1_ffn2_w2, l1_ln_ffn2_g, l1_ln_ffn2_b,
                      emit_bf16=False)
        outs.append(x32)
    return outs[0].reshape(1, s, d) if bsz == 1 else jnp.stack(outs, axis=0)
```

```python
import functools
import math

import numpy as np
import jax
import jax.numpy as jnp
from jax import lax
from jax.experimental import pallas as pl
from jax.experimental.pallas import tpu as pltpu

F32 = jnp.float32
BF16 = jnp.bfloat16

DEPTH = 2
ALPHA = (2 * DEPTH) ** 0.25
LN_EPS = 1e-5
RMS_EPS = 1e-6
HEAD_DIM = 128
MLA_ROPE_DIM = 64
ROPE_BASE = 10000.0
POOL_WINDOWS = (2, 4, 8, 16)
CONV_TAPS = 31

V7X_VMEM_LIMIT_BYTES = 56 * 1024 * 1024
LANES = 128
SUBLANES = 8
POOL_HALO = 16
CONV_HALO = 32
GATE_LANE_STRIDE = 32
FLASH_BLOCK = 1024
LN_ROWS = 64
LOG2E = math.log2(math.e)


def _pick(n, pref, align):
    best = None
    for d in range(align, min(n, pref) + 1, align):
        if n % d == 0:
            best = d
    return n if best is None else best


def _cparams(*sem):
    return pltpu.CompilerParams(dimension_semantics=sem, vmem_limit_bytes=V7X_VMEM_LIMIT_BYTES)


def _split3(x):
    hi = x.astype(BF16)
    r1 = x - hi.astype(F32)
    mid = r1.astype(BF16)
    lo = (r1 - mid.astype(F32)).astype(BF16)
    return hi, mid, lo


def _gateup_body(x_ref, w1_ref, w3_ref, o_ref, *, f):
    tm, tn = o_ref.shape
    x = x_ref[...]
    a = jnp.dot(x, w1_ref[...], preferred_element_type=F32)
    b = jnp.dot(x, w3_ref[...], preferred_element_type=F32)
    hidden = a * jax.nn.sigmoid(a) * b
    col = pl.program_id(1) * tn + lax.broadcasted_iota(jnp.int32, (tm, tn), 1)
    o_ref[...] = jnp.where(col < f, hidden, 0.0).astype(o_ref.dtype)


def _ffn_gateup(xb, w1, w3, fp):
    s, d = xb.shape
    f = w1.shape[1]
    tm = _pick(s, 1024, SUBLANES)
    tn = _pick(fp, 512, LANES)
    return pl.pallas_call(
        functools.partial(_gateup_body, f=f),
        grid=(s // tm, fp // tn),
        in_specs=[pl.BlockSpec((tm, d), lambda i, j: (i, 0)),
                  pl.BlockSpec((d, tn), lambda i, j: (0, j)),
                  pl.BlockSpec((d, tn), lambda i, j: (0, j))],
        out_specs=pl.BlockSpec((tm, tn), lambda i, j: (i, j)),
        out_shape=jax.ShapeDtypeStruct((s, fp), BF16),
        compiler_params=_cparams("parallel", "arbitrary"),
        name="ffn_gateup",
    )(xb, w1, w3)


def _proj_ln_body(y_ref, w_ref, x_ref, g_ref, b_ref, o32_ref, *rest, r, nk, rows, ncols, emit_bf16):
    o16_ref = rest[0] if emit_bf16 else None
    k = pl.program_id(1)
    @pl.when(k == 0)
    def _():
        o32_ref[...] = jnp.zeros(o32_ref.shape, F32)

    y = y_ref[...]
    d = o32_ref.shape[1]
    for c0 in range(0, d, ncols):
        o32_ref[:, pl.ds(c0, ncols)] += jnp.dot(y, w_ref[:, pl.ds(c0, ncols)],
                                                preferred_element_type=F32)

    @pl.when(k == nk - 1)
    def _():
        g = g_ref[...]
        b = b_ref[...]
        tm = o32_ref.shape[0]

        def chunk(c, carry):
            r0 = pl.multiple_of(c * rows, rows)
            z = ALPHA * x_ref[pl.ds(r0, rows), :] + r * o32_ref[pl.ds(r0, rows), :]
            mu = jnp.mean(z, axis=-1, keepdims=True)
            zc = z - mu
            var = jnp.mean(zc * zc, axis=-1, keepdims=True)
            out = zc * lax.rsqrt(var + LN_EPS) * g + b
            o32_ref[pl.ds(r0, rows), :] = out
            if emit_bf16:
                o16_ref[pl.ds(r0, rows), :] = out.astype(BF16)
            return carry

        lax.fori_loop(0, tm // rows, chunk, 0)


def _proj_ln(y, w, x, g, b, r, emit_bf16=True):
    s, kdim = y.shape
    d = w.shape[1]
    tm = _pick(s, 512, SUBLANES)
    tk = _pick(kdim, 512, LANES)
    nk = kdim // tk
    rows = _pick(tm, LN_ROWS, SUBLANES)
    ncols = _pick(d, 1024, LANES)
    out_shape = [jax.ShapeDtypeStruct((s, d), F32)]
    out_specs = [pl.BlockSpec((tm, d), lambda i, k: (i, 0))]
    if emit_bf16:
        out_shape.append(jax.ShapeDtypeStruct((s, d), BF16))
        out_specs.append(pl.BlockSpec((tm, d), lambda i, k: (i, 0)))
    outs = pl.pallas_call(
        functools.partial(_proj_ln_body, r=r, nk=nk, rows=rows, ncols=ncols, emit_bf16=emit_bf16),
        grid=(s // tm, nk),
        in_specs=[pl.BlockSpec((tm, tk), lambda i, k: (i, k)),
                  pl.BlockSpec((tk, d), lambda i, k: (k, 0)),
                  pl.BlockSpec((tm, d), lambda i, k: (i, 0)),
                  pl.BlockSpec((1, d), lambda i, k: (0, 0)),
                  pl.BlockSpec((1, d), lambda i, k: (0, 0))],
        out_specs=out_specs,
        out_shape=out_shape,
        compiler_params=_cparams("parallel", "arbitrary"),
        name="proj_residual_ln",
    )(y, w, x, g.reshape(1, d), b.reshape(1, d))
    return outs if emit_bf16 else (outs[0], None)


def _proj_body(x_ref, w_ref, o_ref, *, scale, scaled_blocks):
    acc = jnp.dot(x_ref[...], w_ref[...], preferred_element_type=F32)
    if scaled_blocks:
        acc = acc * jnp.where(pl.program_id(1) < scaled_blocks, scale, 1.0).astype(F32)
    o_ref[...] = acc.astype(o_ref.dtype)


def _proj(xb, w, out_dtype, tn_pref, scale=1.0, scaled_cols=0):
    s, d = xb.shape
    n = w.shape[1]
    tm = _pick(s, 1024, SUBLANES)
    tn = _pick(n, tn_pref, LANES)
    if scaled_cols:
        tn = math.gcd(tn, scaled_cols)
    return pl.pallas_call(
        functools.partial(_proj_body, scale=scale, scaled_blocks=scaled_cols // tn),
        grid=(s // tm, n // tn),
        in_specs=[pl.BlockSpec((tm, d), lambda i, j: (i, 0)),
                  pl.BlockSpec((d, tn), lambda i, j: (0, j))],
        out_specs=pl.BlockSpec((tm, tn), lambda i, j: (i, j)),
        out_shape=jax.ShapeDtypeStruct((s, n), out_dtype),
        compiler_params=_cparams("parallel", "arbitrary"),
        name="proj",
    )(xb, w)


def _gates_body(f_ref, b_ref, o_ref, carry_ref, *, heads):
    i = pl.program_id(0)
    tm = f_ref.shape[0]

    @pl.when(i == 0)
    def _():
        carry_ref[...] = jnp.zeros_like(carry_ref)

    lane = lax.broadcasted_iota(jnp.int32, (tm, LANES), 1)
    xv = f_ref[...] + b_ref[...]
    logf = jnp.minimum(xv, 0.0) - jnp.log1p(jnp.exp(-jnp.abs(xv)))
    logf = jnp.where(lane < heads, logf, 0.0)
    row = lax.broadcasted_iota(jnp.int32, (tm, tm), 0)
    col = lax.broadcasted_iota(jnp.int32, (tm, tm), 1)
    tril = jnp.where(row >= col, 1.0, 0.0).astype(BF16)
    hi, mid, lo = _split3(logf)
    c = (jnp.dot(tril, hi, preferred_element_type=F32)
         + jnp.dot(tril, mid, preferred_element_type=F32)
         + jnp.dot(tril, lo, preferred_element_type=F32)) + carry_ref[...]
    carry_ref[...] = c[tm - 1:tm, :]
    nhi, nmid, nlo = _split3(-LOG2E * c)
    packed = jnp.where(lane < GATE_LANE_STRIDE, nhi.astype(F32),
                       jnp.where(lane < 2 * GATE_LANE_STRIDE,
                                 pltpu.roll(nmid.astype(F32), GATE_LANE_STRIDE, 1),
                                 pltpu.roll(nlo.astype(F32), 2 * GATE_LANE_STRIDE, 1)))
    o_ref[...] = packed.astype(BF16)


def _fox_gates(uf, ucols, b_f_pad, heads):
    s = uf.shape[0]
    tm = _pick(s, 512, SUBLANES)
    fblk = ucols // LANES
    return pl.pallas_call(
        functools.partial(_gates_body, heads=heads),
        grid=(s // tm,),
        in_specs=[pl.BlockSpec((tm, LANES), lambda i: (i, fblk)),
                  pl.BlockSpec((1, LANES), lambda i: (0, 0))],
        out_specs=pl.BlockSpec((tm, LANES), lambda i: (i, 0)),
        out_shape=jax.ShapeDtypeStruct((s, LANES), BF16),
        scratch_shapes=[pltpu.VMEM((1, LANES), F32)],
        compiler_params=_cparams("arbitrary"),
        name="fox_gates",
    )(uf, b_f_pad)


def _flash_body(*refs, tb, fox, rchunk):
    n_in = 4 if fox else 5
    q1_ref = refs[0]
    q2_ref = None if fox else refs[1]
    k1_ref, k2_ref, v_ref = refs[n_in - 3:n_in]
    o_ref = refs[n_in]
    q_sc, s0, s1, p0, p1, a0, a1, m_sc, acc_sc = refs[n_in + 1:]
    s_bufs, p_bufs, a_bufs = (s0, s1), (p0, p1), (a0, a1)
    h = pl.program_id(0)
    qi = pl.program_id(1)

    q_sc[:, pl.ds(0, HEAD_DIM)] = q1_ref[...]
    if fox:
        lane = lax.broadcasted_iota(jnp.int32, (tb, LANES), 1)
        sel = (lane < 3 * GATE_LANE_STRIDE) & (jnp.bitwise_and(lane, GATE_LANE_STRIDE - 1) == h)
        q_sc[:, pl.ds(HEAD_DIM, LANES)] = jnp.where(sel, 1.0, 0.0).astype(BF16)
    else:
        q_sc[:, pl.ds(HEAD_DIM, LANES)] = q2_ref[...]
    m_sc[...] = jnp.full(m_sc.shape, -jnp.inf, F32)
    acc_sc[...] = jnp.zeros(acc_sc.shape, F32)
    p1[...] = jnp.zeros(p1.shape, BF16)
    a1[...] = jnp.ones(a1.shape, F32)
    ones = jnp.ones((tb, LANES), BF16)

    def qk(b, s_dst):
        k0 = pl.multiple_of(b * tb, tb)
        kk = jnp.concatenate([k1_ref[pl.ds(k0, tb), :], k2_ref[pl.ds(k0, tb), :]], axis=1)
        s_dst[...] = lax.dot_general(q_sc[...], kk, (((1,), (1,)), ((), ())),
                                     preferred_element_type=F32)

    def pv(b, p_src, a_src):
        k0 = pl.multiple_of(jnp.maximum(b, 0) * tb, tb)
        vv = jnp.concatenate([v_ref[pl.ds(k0, tb), :], ones], axis=1)
        acc_sc[...] = a_src[...] * acc_sc[...] + jnp.dot(p_src[...], vv, preferred_element_type=F32)

    def sm(s_src, p_dst, a_dst, diagonal):
        for r0 in range(0, tb, rchunk):
            s = s_src[pl.ds(r0, rchunk), :]
            if diagonal:
                row = r0 + lax.broadcasted_iota(jnp.int32, (rchunk, tb), 0)
                col = lax.broadcasted_iota(jnp.int32, (rchunk, tb), 1)
                s = jnp.where(row >= col, s, -jnp.inf)
            m_prev = m_sc[pl.ds(r0, rchunk), :]
            m_new = jnp.maximum(m_prev, jnp.max(s, axis=1, keepdims=True))
            a_dst[pl.ds(r0, rchunk), :] = jnp.exp2(m_prev - m_new)
            p_dst[pl.ds(r0, rchunk), :] = jnp.exp2(s - m_new).astype(BF16)
            m_sc[pl.ds(r0, rchunk), :] = m_new

    def half(b, cur):
        nxt = 1 - cur
        qk(b + 1, s_bufs[nxt])
        pv(b - 1, p_bufs[nxt], a_bufs[nxt])
        sm(s_bufs[cur], p_bufs[cur], a_bufs[cur], False)

    def last(cur):
        pv(qi - 1, p_bufs[1 - cur], a_bufs[1 - cur])
        sm(s_bufs[cur], p_bufs[cur], a_bufs[cur], True)
        pv(qi, p_bufs[cur], a_bufs[cur])
        o_ref[...] = (acc_sc[:, pl.ds(0, HEAD_DIM)]
                      / acc_sc[:, pl.ds(HEAD_DIM, LANES)]).astype(o_ref.dtype)

    qk(0, s0)

    def pair(i, carry):
        half(2 * i, 0)
        half(2 * i + 1, 1)
        return carry

    lax.fori_loop(0, lax.shift_right_logical(qi, 1), pair, 0)
    odd = jnp.bitwise_and(qi, 1)

    @pl.when(odd == 1)
    def _():
        half(qi - 1, 0)
        last(1)

    @pl.when(odd == 0)
    def _():
        last(0)


def _flash(q, q1_blk, q2_blk, kv, k1_blk, v_blk, k2, heads, tb_pref=FLASH_BLOCK):
    s = q.shape[0]
    tb = _pick(s, tb_pref, LANES)
    rchunk = _pick(tb, 32, SUBLANES)
    fox = q2_blk is None
    in_specs = [pl.BlockSpec((tb, HEAD_DIM), lambda h, i: (i, q1_blk + h))]
    args = [q]
    if not fox:
        in_specs.append(pl.BlockSpec((tb, HEAD_DIM), lambda h, i: (i, q2_blk + h)))
        args.append(q)
    in_specs += [pl.BlockSpec((s, HEAD_DIM), lambda h, i: (0, k1_blk + h)),
                 pl.BlockSpec((s, LANES), lambda h, i: (0, 0)),
                 pl.BlockSpec((s, HEAD_DIM), lambda h, i: (0, v_blk + h))]
    args += [kv, k2, kv]
    score = pltpu.VMEM((tb, tb), F32)
    prob = pltpu.VMEM((tb, tb), BF16)
    col = pltpu.VMEM((tb, 1), F32)
    return pl.pallas_call(
        functools.partial(_flash_body, tb=tb, fox=fox, rchunk=rchunk),
        grid=(heads, s // tb),
        in_specs=in_specs,
        out_specs=pl.BlockSpec((tb, HEAD_DIM), lambda h, i: (i, h)),
        out_shape=jax.ShapeDtypeStruct((s, heads * HEAD_DIM), BF16),
        scratch_shapes=[pltpu.VMEM((tb, HEAD_DIM + LANES), BF16), score, score, prob, prob,
                        col, col, col, pltpu.VMEM((tb, HEAD_DIM + LANES), F32)],
        compiler_params=_cparams("parallel", "arbitrary"),
        name="flash_fox" if fox else "flash_mla",
    )(*args)


def _pool_body(u_ref, halo_ref, w_ref, sc_ref, o_ref, buf, *, groups, gdim):
    i = pl.program_id(0)
    tm = u_ref.shape[0]
    buf[pl.ds(0, POOL_HALO), :] = jnp.where(i == 0, 0.0, halo_ref[...])
    buf[pl.ds(POOL_HALO, tm), :] = u_ref[...]
    t = (i * tm + lax.broadcasted_iota(jnp.int32, (tm, 1), 0)).astype(F32)
    for g in range(groups):
        win = POOL_WINDOWS[g]
        c0 = g * gdim
        tot = buf[pl.ds(POOL_HALO, tm), pl.ds(c0, gdim)]
        cur = tot
        for lag in range(1, win):
            tot = tot + buf[pl.ds(POOL_HALO - lag, tm), pl.ds(c0, gdim)]
        count = jnp.minimum(t + 1.0, float(win))
        pooled = tot / count - cur
        mixed = jnp.dot(pooled.astype(BF16), w_ref[g], preferred_element_type=F32)
        o_ref[:, pl.ds(c0, gdim)] = (mixed * sc_ref[:, pl.ds(c0, gdim)]).astype(o_ref.dtype)


def _pool(uf, pool_w_b, pool_scale):
    s = uf.shape[0]
    groups, gdim, _ = pool_w_b.shape
    c = groups * gdim
    tm = _pick(s, 256, POOL_HALO)
    hb = tm // POOL_HALO
    return pl.pallas_call(
        functools.partial(_pool_body, groups=groups, gdim=gdim),
        grid=(s // tm,),
        in_specs=[pl.BlockSpec((tm, c), lambda i: (i, 0)),
                  pl.BlockSpec((POOL_HALO, c), lambda i: (jnp.maximum(i * hb - 1, 0), 0)),
                  pl.BlockSpec((groups, gdim, gdim), lambda i: (0, 0, 0)),
                  pl.BlockSpec((1, c), lambda i: (0, 0))],
        out_specs=pl.BlockSpec((tm, c), lambda i: (i, 0)),
        out_shape=jax.ShapeDtypeStruct((s, c), BF16),
        scratch_shapes=[pltpu.VMEM((tm + POOL_HALO, c), F32)],
        compiler_params=_cparams("parallel"),
        name="multiscale_pool",
    )(uf, uf, pool_w_b, pool_scale.reshape(1, c))


def _conv_body(a_ref, g_ref, ha_ref, hg_ref, w_ref, cb_ref, lg_ref, lb_ref, o_ref, buf, ybuf, *, rows, lanes):
    i = pl.program_id(0)
    tm, c = a_ref.shape
    halo = ha_ref[...] * jax.nn.sigmoid(hg_ref[...])
    buf[pl.ds(0, CONV_HALO), :] = jnp.where(i == 0, 0.0, halo)
    buf[pl.ds(CONV_HALO, tm), :] = a_ref[...] * jax.nn.sigmoid(g_ref[...])
    base = CONV_HALO - (CONV_TAPS - 1)
    for r0 in range(0, tm, rows):
        for c0 in range(0, c, lanes):
            acc = jnp.zeros((rows, lanes), F32) + cb_ref[:, pl.ds(c0, lanes)]
            for j in range(CONV_TAPS):
                acc = acc + w_ref[pl.ds(j, 1), pl.ds(c0, lanes)] * buf[pl.ds(base + r0 + j, rows), pl.ds(c0, lanes)]
            ybuf[pl.ds(r0, rows), pl.ds(c0, lanes)] = acc
    y = ybuf[...]
    mu = jnp.mean(y, axis=-1, keepdims=True)
    yc = y - mu
    var = jnp.mean(yc * yc, axis=-1, keepdims=True)
    z = yc * lax.rsqrt(var + LN_EPS) * lg_ref[...] + lb_ref[...]
    o_ref[...] = (z * jax.nn.sigmoid(z)).astype(o_ref.dtype)


def _conformer_conv(h1, cw, conv_w_p, conv_b, ln_g, ln_b):
    s = h1.shape[0]
    tm = _pick(s, 128, CONV_HALO)
    hb = tm // CONV_HALO
    rows = _pick(tm, 32, SUBLANES)
    lanes = _pick(cw, 256, LANES)
    nblk = 1
    vec = lambda a: a.reshape(1, cw)
    return pl.pallas_call(
        functools.partial(_conv_body, rows=rows, lanes=lanes),
        grid=(s // tm,),
        in_specs=[pl.BlockSpec((tm, cw), lambda i: (i, 0)),
                  pl.BlockSpec((tm, cw), lambda i: (i, nblk)),
                  pl.BlockSpec((CONV_HALO, cw), lambda i: (jnp.maximum(i * hb - 1, 0), 0)),
                  pl.BlockSpec((CONV_HALO, cw), lambda i: (jnp.maximum(i * hb - 1, 0), nblk)),
                  pl.BlockSpec((CONV_HALO, cw), lambda i: (0, 0)),
                  pl.BlockSpec((1, cw), lambda i: (0, 0)),
                  pl.BlockSpec((1, cw), lambda i: (0, 0)),
                  pl.BlockSpec((1, cw), lambda i: (0, 0))],
        out_specs=pl.BlockSpec((tm, cw), lambda i: (i, 0)),
        out_shape=jax.ShapeDtypeStruct((s, cw), BF16),
        scratch_shapes=[pltpu.VMEM((tm + CONV_HALO, cw), F32), pltpu.VMEM((tm, cw), F32)],
        compiler_params=_cparams("parallel"),
        name="conformer_conv",
    )(h1, h1, h1, h1, conv_w_p, vec(conv_b), vec(ln_g), vec(ln_b))


def _rope128(r, cosv, sinv, lane):
    half = MLA_ROPE_DIM // 2
    s1 = jnp.where(lane < half, -sinv, 0.0)
    s2 = jnp.where((lane >= half) & (lane < 2 * half), sinv, 0.0)
    return r * cosv + pltpu.roll(r, LANES - half, 1) * s1 + pltpu.roll(r, half, 1) * s2


def _rope_k_body(pos_ref, invf_ref, kpe_ref, cos_ref, sin_ref, k2_ref):
    tm = pos_ref.shape[0]
    ang = pos_ref[...].astype(F32) * invf_ref[...]
    cosv = jnp.cos(ang)
    sinv = jnp.sin(ang)
    cos_ref[...] = cosv
    sin_ref[...] = sinv
    lane = lax.broadcasted_iota(jnp.int32, (tm, LANES), 1)
    k2_ref[...] = _rope128(kpe_ref[...], cosv, sinv, lane).astype(BF16)


def _rope_k(positions_col, invf, h1, kpe_block):
    s = h1.shape[0]
    tm = _pick(s, 512, SUBLANES)
    tab = jax.ShapeDtypeStruct((s, LANES), F32)
    return pl.pallas_call(
        _rope_k_body,
        grid=(s // tm,),
        in_specs=[pl.BlockSpec((tm, 1), lambda i: (i, 0)),
                  pl.BlockSpec((1, LANES), lambda i: (0, 0)),
                  pl.BlockSpec((tm, LANES), lambda i: (i, kpe_block))],
        out_specs=[pl.BlockSpec((tm, LANES), lambda i: (i, 0))] * 3,
        out_shape=[tab, tab, jax.ShapeDtypeStruct((s, LANES), BF16)],
        compiler_params=_cparams("parallel"),
        name="rope_tables_key",
    )(positions_col, invf, h1)


def _rms_proj_body(c_ref, g_ref, w_ref, cos_ref, sin_ref, o_ref, xn_ref, *, scale, rope_from):
    j = pl.program_id(1)
    tm, tn = o_ref.shape

    @pl.when(j == 0)
    def _():
        xf = c_ref[...]
        ms = jnp.mean(xf * xf, axis=-1, keepdims=True)
        xn_ref[...] = (xf * lax.rsqrt(ms + RMS_EPS) * g_ref[...]).astype(BF16)

    acc = jnp.dot(xn_ref[...], w_ref[...], preferred_element_type=F32)

    def plain():
        o_ref[...] = (acc * scale).astype(o_ref.dtype)

    if rope_from is None:
        plain()
        return

    pl.when(j < rope_from)(plain)

    @pl.when(j >= rope_from)
    def _():
        lane = lax.broadcasted_iota(jnp.int32, (tm, LANES), 1)
        cosv = cos_ref[...]
        sinv = sin_ref[...]
        for c0 in range(0, tn, LANES):
            r = _rope128(acc[:, c0:c0 + LANES], cosv, sinv, lane)
            o_ref[:, pl.ds(c0, LANES)] = (r * scale).astype(o_ref.dtype)


def _rms_proj(h1, cblock, rank, g, w, cosv, sinv, scale, rope_cols):
    s = h1.shape[0]
    n = w.shape[1]
    tm = _pick(s, 1024, SUBLANES)
    tn = _pick(n, 512, LANES)
    if rope_cols:
        tn = math.gcd(tn, n - rope_cols)
    rope_from = (n - rope_cols) // tn if rope_cols else None
    return pl.pallas_call(
        functools.partial(_rms_proj_body, scale=scale, rope_from=rope_from),
        grid=(s // tm, n // tn),
        in_specs=[pl.BlockSpec((tm, rank), lambda i, j: (i, cblock)),
                  pl.BlockSpec((1, rank), lambda i, j: (0, 0)),
                  pl.BlockSpec((rank, tn), lambda i, j: (0, j)),
                  pl.BlockSpec((tm, LANES), lambda i, j: (i, 0)),
                  pl.BlockSpec((tm, LANES), lambda i, j: (i, 0))],
        out_specs=pl.BlockSpec((tm, tn), lambda i, j: (i, j)),
        out_shape=jax.ShapeDtypeStruct((s, n), BF16),
        scratch_shapes=[pltpu.VMEM((tm, rank), BF16)],
        compiler_params=_cparams("parallel", "arbitrary"),
        name="rms_up_proj",
    )(h1, g.reshape(1, rank), w, cosv, sinv)


def _ffn(x32, x16, w1, w3, w2, g, b, emit_bf16=True):
    d, f = w1.shape
    fp = -(-f // 512) * 512
    w2p = jnp.pad(w2.astype(BF16), ((0, fp - f), (0, 0)))
    hidden = _ffn_gateup(x16, w1.astype(BF16), w3.astype(BF16), fp)
    return _proj_ln(hidden, w2p, x32, g, b, 0.5, emit_bf16)


def _mixer_fox_pool(x32, x16, w_in, b_f, pool_w, pool_scale, w_out, g, b):
    d = x32.shape[1]
    heads = b_f.shape[0]
    fw = heads * HEAD_DIM
    groups, gdim, _ = pool_w.shape
    pw = groups * gdim
    w_qkv = w_in[:, :3 * fw].astype(BF16)
    w_uf = jnp.concatenate([w_in[:, 3 * fw + heads:], w_in[:, 3 * fw:3 * fw + heads],
                            jnp.zeros((d, LANES - heads), F32)], axis=1).astype(BF16)
    b_f_pad = jnp.pad(b_f, (0, LANES - heads)).reshape(1, LANES)

    qkv = _proj(x16, w_qkv, BF16, 1024, scale=LOG2E * HEAD_DIM ** -0.5, scaled_cols=fw)
    uf = _proj(x16, w_uf, F32, pw + LANES)
    gate_parts = _fox_gates(uf, pw, b_f_pad, heads)
    y_a = _flash(qkv, 0, None, qkv, heads, 2 * heads, gate_parts, heads)
    y_b = _pool(uf, pool_w.astype(BF16), pool_scale)
    y = jnp.concatenate([y_a, y_b], axis=1)
    return _proj_ln(y, w_out.astype(BF16), x32, g, b, 1.0)


def _mixer_conv_mla(x32, x16, positions, w_in, conv_w, conv_b, conv_ln_g, conv_ln_b,
                    q_norm_g, w_uq, kv_norm_g, w_ukv, w_out, g, b):
    s, d = x32.shape
    cw = conv_b.shape[0]
    q_rank = q_norm_g.shape[0]
    kv_rank = kv_norm_g.shape[0]
    qk_dim = HEAD_DIM + MLA_ROPE_DIM
    heads = w_uq.shape[1] // qk_dim
    hw = heads * HEAD_DIM
    n_in = w_in.shape[1]
    n_in_p = -(-(n_in + MLA_ROPE_DIM) // 768) * 768
    w_in_p = jnp.pad(w_in.astype(BF16), ((0, 0), (0, n_in_p - n_in)))
    c_q_off = 2 * cw
    c_kv_off = c_q_off + q_rank
    kpe_off = c_kv_off + kv_rank
    assert c_q_off % q_rank == 0 and c_kv_off % kv_rank == 0 and kpe_off % LANES == 0

    wq = w_uq.reshape(q_rank, heads, qk_dim)
    wq_rope = jnp.pad(wq[:, :, HEAD_DIM:], ((0, 0), (0, 0), (0, LANES - MLA_ROPE_DIM)))
    wq_p = jnp.concatenate([wq[:, :, :HEAD_DIM].reshape(q_rank, hw),
                            wq_rope.reshape(q_rank, heads * LANES)], axis=1).astype(BF16)
    wkv = w_ukv.reshape(kv_rank, heads, 2 * HEAD_DIM)
    wkv_p = jnp.concatenate([wkv[:, :, :HEAD_DIM].reshape(kv_rank, hw),
                             wkv[:, :, HEAD_DIM:].reshape(kv_rank, hw)], axis=1).astype(BF16)
    conv_w_p = jnp.pad(conv_w.reshape(CONV_TAPS, cw), ((0, CONV_HALO - CONV_TAPS), (0, 0)))
    half = MLA_ROPE_DIM // 2
    invf = np.zeros((1, LANES), np.float32)
    freqs = ROPE_BASE ** (-np.arange(half, dtype=np.float32) / half)
    invf[0, :half] = freqs
    invf[0, half:2 * half] = freqs
    invf = jnp.asarray(invf)

    h1 = _proj(x16, w_in_p, F32, 768)
    y_c = _conformer_conv(h1, cw, conv_w_p, conv_b, conv_ln_g, conv_ln_b)
    cosv, sinv, k2 = _rope_k(positions.reshape(s, 1), invf, h1, kpe_off // LANES)
    q = _rms_proj(h1, c_q_off // q_rank, q_rank, q_norm_g, wq_p, cosv, sinv,
                  LOG2E * qk_dim ** -0.5, heads * LANES)
    kv = _rms_proj(h1, c_kv_off // kv_rank, kv_rank, kv_norm_g, wkv_p, cosv, sinv, 1.0, 0)
    y_d = _flash(q, 0, heads, kv, 0, heads, k2, heads)
    y = jnp.concatenate([y_c, y_d], axis=1)
    return _proj_ln(y, w_out.astype(BF16), x32, g, b, 1.0)


def kernel(x, positions, l0_ffn1_w1, l0_ffn1_w3, l0_ffn1_w2, l0_ln_ffn1_g, l0_ln_ffn1_b, l0_w_in, l0_b_f, l0_pool_w, l0_pool_scale, l0_w_out, l0_ln_mix_g, l0_ln_mix_b, l0_ffn2_w1, l0_ffn2_w3, l0_ffn2_w2, l0_ln_ffn2_g, l0_ln_ffn2_b, l1_ffn1_w1, l1_ffn1_w3, l1_ffn1_w2, l1_ln_ffn1_g, l1_ln_ffn1_b, l1_w_in, l1_conv_w, l1_conv_b, l1_conv_ln_g, l1_conv_ln_b, l1_q_norm_g, l1_w_uq, l1_kv_norm_g, l1_w_ukv, l1_w_out, l1_ln_mix_g, l1_ln_mix_b, l1_ffn2_w1, l1_ffn2_w3, l1_ffn2_w2, l1_ln_ffn2_g, l1_ln_ffn2_b):
    bsz, s, d = x.shape
    outs = []
    for bi in range(bsz):
        x32 = x.reshape(s, d) if bsz == 1 else x[bi]
        x16 = x32.astype(BF16)
        pos = positions.reshape(s) if bsz == 1 else positions[bi]
        x32, x16 = _ffn(x32, x16, l0_ffn1_w1, l0_ffn1_w3, l0_ffn1_w2, l0_ln_ffn1_g, l0_ln_ffn1_b)
        x32, x16 = _mixer_fox_pool(x32, x16, l0_w_in, l0_b_f, l0_pool_w, l0_pool_scale, l0_w_out,
                                   l0_ln_mix_g, l0_ln_mix_b)
        x32, x16 = _ffn(x32, x16, l0_ffn2_w1, l0_ffn2_w3, l0_ffn2_w2, l0_ln_ffn2_g, l0_ln_ffn2_b)
        x32, x16 = _ffn(x32, x16, l1_ffn1_w1, l1_ffn1_w3, l1_ffn1_w2, l1_ln_ffn1_g, l1_ln_ffn1_b)
        x32, x16 = _mixer_conv_mla(x32, x16, pos, l1_w_in, l1_conv_w, l1_conv_b, l1_conv_ln_g,
                                   l1_conv_ln_b, l1_q_norm_g, l1_w_uq, l1_kv_norm_g, l1_w_ukv,
                                   l1_w_out, l1_ln_mix_g, l1_ln_mix_b)
        x32, _ = _ffn(x32, x16, l1_ffn2_w1, l1_ffn2_w3, l1_ffn2_w2, l1_ln_ffn2_g, l1_ln_ffn2_b,
                      emit_bf16=False)
        outs.append(x32)
    return outs[0].reshape(1, s, d) if bsz == 1 else jnp.stack(outs, axis=0)
```

```python
import functools
import math

import numpy as np
import jax
import jax.numpy as jnp
from jax import lax
from jax.experimental import pallas as pl
from jax.experimental.pallas import tpu as pltpu

F32 = jnp.float32
BF16 = jnp.bfloat16

DEPTH = 2
ALPHA = (2 * DEPTH) ** 0.25
LN_EPS = 1e-5
RMS_EPS = 1e-6
HEAD_DIM = 128
MLA_ROPE_DIM = 64
ROPE_BASE = 10000.0
POOL_WINDOWS = (2, 4, 8, 16)
CONV_TAPS = 31

V7X_VMEM_LIMIT_BYTES = 56 * 1024 * 1024
LANES = 128
SUBLANES = 8
POOL_HALO = 16
CONV_HALO = 32
GATE_LANE_STRIDE = 32
FLASH_BLOCK = 1024
LN_ROWS = 128
LOG2E = math.log2(math.e)


def _pick(n, pref, align):
    best = None
    for d in range(align, min(n, pref) + 1, align):
        if n % d == 0:
            best = d
    return n if best is None else best


def _cparams(*sem):
    return pltpu.CompilerParams(dimension_semantics=sem, vmem_limit_bytes=V7X_VMEM_LIMIT_BYTES)


def _split3(x):
    hi = x.astype(BF16)
    r1 = x - hi.astype(F32)
    mid = r1.astype(BF16)
    lo = (r1 - mid.astype(F32)).astype(BF16)
    return hi, mid, lo


def _gateup_body(x_ref, w1_ref, w3_ref, o_ref, *, f):
    tm, tn = o_ref.shape
    x = x_ref[...]
    a = jnp.dot(x, w1_ref[...], preferred_element_type=F32)
    b = jnp.dot(x, w3_ref[...], preferred_element_type=F32)
    hidden = a * jax.nn.sigmoid(a) * b
    col = pl.program_id(1) * tn + lax.broadcasted_iota(jnp.int32, (tm, tn), 1)
    o_ref[...] = jnp.where(col < f, hidden, 0.0).astype(o_ref.dtype)


def _ffn_gateup(xb, w1, w3, fp):
    s, d = xb.shape
    f = w1.shape[1]
    tm = _pick(s, 1024, SUBLANES)
    tn = _pick(fp, 512, LANES)
    return pl.pallas_call(
        functools.partial(_gateup_body, f=f),
        grid=(s // tm, fp // tn),
        in_specs=[pl.BlockSpec((tm, d), lambda i, j: (i, 0)),
                  pl.BlockSpec((d, tn), lambda i, j: (0, j)),
                  pl.BlockSpec((d, tn), lambda i, j: (0, j))],
        out_specs=pl.BlockSpec((tm, tn), lambda i, j: (i, j)),
        out_shape=jax.ShapeDtypeStruct((s, fp), BF16),
        compiler_params=_cparams("parallel", "arbitrary"),
        name="ffn_gateup",
    )(xb, w1, w3)


def _proj_ln_body(*refs, r, nk, nk_first, rows, ncols, emit_bf16):
    two_lhs = nk_first < nk
    ya_ref = refs[0]
    yb_ref = refs[1] if two_lhs else None
    w_ref, x_ref, g_ref, b_ref, o32_ref = refs[1 + two_lhs:6 + two_lhs]
    o16_ref = refs[6 + two_lhs] if emit_bf16 else None
    k = pl.program_id(1)

    @pl.when(k == 0)
    def _():
        o32_ref[...] = jnp.zeros(o32_ref.shape, F32)

    y = ya_ref[...]
    if two_lhs:
        y = jnp.where(k < nk_first, y, yb_ref[...])
    d = o32_ref.shape[1]
    for c0 in range(0, d, ncols):
        o32_ref[:, pl.ds(c0, ncols)] += jnp.dot(y, w_ref[:, pl.ds(c0, ncols)],
                                                preferred_element_type=F32)

    @pl.when(k == nk - 1)
    def _():
        g = g_ref[...]
        b = b_ref[...]
        tm = o32_ref.shape[0]

        def chunk(c, carry):
            r0 = pl.multiple_of(c * rows, rows)
            z = ALPHA * x_ref[pl.ds(r0, rows), :] + r * o32_ref[pl.ds(r0, rows), :]
            mu = jnp.mean(z, axis=-1, keepdims=True)
            zc = z - mu
            var = jnp.mean(zc * zc, axis=-1, keepdims=True)
            out = zc * lax.rsqrt(var + LN_EPS) * g + b
            o32_ref[pl.ds(r0, rows), :] = out
            if emit_bf16:
                o16_ref[pl.ds(r0, rows), :] = out.astype(BF16)
            return carry

        lax.fori_loop(0, tm // rows, chunk, 0)


def _proj_ln(ys, w, x, g, b, r, emit_bf16=True):
    s = ys[0].shape[0]
    kdim, d = w.shape
    tm = _pick(s, 512, SUBLANES)
    tk = _pick(math.gcd(*[y.shape[1] for y in ys]), 512, LANES)
    nk = kdim // tk
    nk_first = ys[0].shape[1] // tk
    y_specs = [pl.BlockSpec((tm, tk), lambda i, k: (i, jnp.minimum(k, nk_first - 1)))]
    if len(ys) == 2:
        y_specs.append(pl.BlockSpec((tm, tk), lambda i, k: (i, jnp.maximum(k - nk_first, 0))))
    rows = _pick(tm, LN_ROWS, SUBLANES)
    ncols = _pick(d, 1024, LANES)
    out_shape = [jax.ShapeDtypeStruct((s, d), F32)]
    out_specs = [pl.BlockSpec((tm, d), lambda i, k: (i, 0))]
    if emit_bf16:
        out_shape.append(jax.ShapeDtypeStruct((s, d), BF16))
        out_specs.append(pl.BlockSpec((tm, d), lambda i, k: (i, 0)))
    outs = pl.pallas_call(
        functools.partial(_proj_ln_body, r=r, nk=nk, nk_first=nk_first, rows=rows, ncols=ncols,
                          emit_bf16=emit_bf16),
        grid=(s // tm, nk),
        in_specs=y_specs + [pl.BlockSpec((tk, d), lambda i, k: (k, 0)),
                            pl.BlockSpec((tm, d), lambda i, k: (i, 0)),
                            pl.BlockSpec((1, d), lambda i, k: (0, 0)),
                            pl.BlockSpec((1, d), lambda i, k: (0, 0))],
        out_specs=out_specs,
        out_shape=out_shape,
        compiler_params=_cparams("parallel", "arbitrary"),
        name="proj_residual_ln",
    )(*ys, w, x, g.reshape(1, d), b.reshape(1, d))
    return outs if emit_bf16 else (outs[0], None)


def _proj_body(x_ref, w_ref, o_ref, *, scale, scaled_blocks):
    acc = jnp.dot(x_ref[...], w_ref[...], preferred_element_type=F32)
    if scaled_blocks:
        acc = acc * jnp.where(pl.program_id(1) < scaled_blocks, scale, 1.0).astype(F32)
    o_ref[...] = acc.astype(o_ref.dtype)


def _proj(xb, w, out_dtype, tn_pref, scale=1.0, scaled_cols=0):
    s, d = xb.shape
    n = w.shape[1]
    tm = _pick(s, 1024, SUBLANES)
    tn = _pick(n, tn_pref, LANES)
    if scaled_cols:
        tn = math.gcd(tn, scaled_cols)
    return pl.pallas_call(
        functools.partial(_proj_body, scale=scale, scaled_blocks=scaled_cols // tn),
        grid=(s // tm, n // tn),
        in_specs=[pl.BlockSpec((tm, d), lambda i, j: (i, 0)),
                  pl.BlockSpec((d, tn), lambda i, j: (0, j))],
        out_specs=pl.BlockSpec((tm, tn), lambda i, j: (i, j)),
        out_shape=jax.ShapeDtypeStruct((s, n), out_dtype),
        compiler_params=_cparams("parallel", "arbitrary"),
        name="proj",
    )(xb, w)


def _gates_body(f_ref, b_ref, o_ref, carry_ref, *, heads):
    i = pl.program_id(0)
    tm = f_ref.shape[0]

    @pl.when(i == 0)
    def _():
        carry_ref[...] = jnp.zeros_like(carry_ref)

    lane = lax.broadcasted_iota(jnp.int32, (tm, LANES), 1)
    xv = f_ref[...] + b_ref[...]
    logf = jnp.minimum(xv, 0.0) - jnp.log1p(jnp.exp(-jnp.abs(xv)))
    logf = jnp.where(lane < heads, logf, 0.0)
    row = lax.broadcasted_iota(jnp.int32, (tm, tm), 0)
    col = lax.broadcasted_iota(jnp.int32, (tm, tm), 1)
    tril = jnp.where(row >= col, 1.0, 0.0).astype(BF16)
    hi, mid, lo = _split3(logf)
    c = (jnp.dot(tril, hi, preferred_element_type=F32)
         + jnp.dot(tril, mid, preferred_element_type=F32)
         + jnp.dot(tril, lo, preferred_element_type=F32)) + carry_ref[...]
    carry_ref[...] = c[tm - 1:tm, :]
    nhi, nmid, nlo = _split3(-LOG2E * c)
    packed = jnp.where(lane < GATE_LANE_STRIDE, nhi.astype(F32),
                       jnp.where(lane < 2 * GATE_LANE_STRIDE,
                                 pltpu.roll(nmid.astype(F32), GATE_LANE_STRIDE, 1),
                                 pltpu.roll(nlo.astype(F32), 2 * GATE_LANE_STRIDE, 1)))
    o_ref[...] = packed.astype(BF16)


def _fox_gates(uf, ucols, b_f_pad, heads):
    s = uf.shape[0]
    tm = _pick(s, 512, SUBLANES)
    fblk = ucols // LANES
    return pl.pallas_call(
        functools.partial(_gates_body, heads=heads),
        grid=(s // tm,),
        in_specs=[pl.BlockSpec((tm, LANES), lambda i: (i, fblk)),
                  pl.BlockSpec((1, LANES), lambda i: (0, 0))],
        out_specs=pl.BlockSpec((tm, LANES), lambda i: (i, 0)),
        out_shape=jax.ShapeDtypeStruct((s, LANES), BF16),
        scratch_shapes=[pltpu.VMEM((1, LANES), F32)],
        compiler_params=_cparams("arbitrary"),
        name="fox_gates",
    )(uf, b_f_pad)


def _flash_body(*refs, tb, fox, rchunk):
    n_in = 4 if fox else 5
    q1_ref = refs[0]
    q2_ref = None if fox else refs[1]
    k1_ref, k2_ref, v_ref = refs[n_in - 3:n_in]
    o_ref = refs[n_in]
    q_sc, s0, s1, p0, p1, a0, a1, m_sc, acc_sc = refs[n_in + 1:]
    s_bufs, p_bufs, a_bufs = (s0, s1), (p0, p1), (a0, a1)
    h = pl.program_id(0)
    qi = pl.program_id(1)

    q_sc[:, pl.ds(0, HEAD_DIM)] = q1_ref[...]
    if fox:
        lane = lax.broadcasted_iota(jnp.int32, (tb, LANES), 1)
        sel = (lane < 3 * GATE_LANE_STRIDE) & (jnp.bitwise_and(lane, GATE_LANE_STRIDE - 1) == h)
        q_sc[:, pl.ds(HEAD_DIM, LANES)] = jnp.where(sel, 1.0, 0.0).astype(BF16)
    else:
        q_sc[:, pl.ds(HEAD_DIM, LANES)] = q2_ref[...]
    m_sc[...] = jnp.full(m_sc.shape, -jnp.inf, F32)
    acc_sc[...] = jnp.zeros(acc_sc.shape, F32)
    p1[...] = jnp.zeros(p1.shape, BF16)
    a1[...] = jnp.ones(a1.shape, F32)
    ones = jnp.ones((tb, LANES), BF16)
    hb = tb // 2

    def qk(b, s_dst):
        k0 = pl.multiple_of(b * tb, tb)
        kk = jnp.concatenate([k1_ref[pl.ds(k0, tb), :], k2_ref[pl.ds(k0, tb), :]], axis=1)
        s_dst[...] = lax.dot_general(q_sc[...], kk, (((1,), (1,)), ((), ())),
                                     preferred_element_type=F32)

    def pv(b, p_src, a_src):
        k0 = pl.multiple_of(jnp.maximum(b, 0) * tb, tb)
        vv = jnp.concatenate([v_ref[pl.ds(k0, tb), :], ones], axis=1)
        acc_sc[...] = a_src[...] * acc_sc[...] + jnp.dot(p_src[...], vv, preferred_element_type=F32)

    def sm(s_src, p_dst, a_dst, diagonal):
        for r0 in range(0, tb, rchunk):
            ncol = hb if diagonal and r0 + rchunk <= hb else tb
            s = s_src[pl.ds(r0, rchunk), pl.ds(0, ncol)]
            if diagonal:
                row = r0 + lax.broadcasted_iota(jnp.int32, (rchunk, ncol), 0)
                col = lax.broadcasted_iota(jnp.int32, (rchunk, ncol), 1)
                s = jnp.where(row >= col, s, -jnp.inf)
            m_prev = m_sc[pl.ds(r0, rchunk), :]
            m_new = jnp.maximum(m_prev, jnp.max(s, axis=1, keepdims=True))
            a_dst[pl.ds(r0, rchunk), :] = jnp.exp2(m_prev - m_new)
            p_dst[pl.ds(r0, rchunk), pl.ds(0, ncol)] = jnp.exp2(s - m_new).astype(BF16)
            m_sc[pl.ds(r0, rchunk), :] = m_new

    def pv_diagonal(b, p_src, a_src):
        k0 = pl.multiple_of(b * tb, tb)
        vv = jnp.concatenate([v_ref[pl.ds(k0, tb), :], ones], axis=1)
        acc_sc[pl.ds(0, hb), :] = (a_src[pl.ds(0, hb), :] * acc_sc[pl.ds(0, hb), :]
                                   + jnp.dot(p_src[pl.ds(0, hb), pl.ds(0, hb)], vv[:hb],
                                             preferred_element_type=F32))
        acc_sc[pl.ds(hb, hb), :] = (a_src[pl.ds(hb, hb), :] * acc_sc[pl.ds(hb, hb), :]
                                    + jnp.dot(p_src[pl.ds(hb, hb), :], vv,
                                              preferred_element_type=F32))

    def half(b, cur):
        nxt = 1 - cur
        pv(b - 1, p_bufs[nxt], a_bufs[nxt])
        qk(b + 1, s_bufs[nxt])
        sm(s_bufs[cur], p_bufs[cur], a_bufs[cur], False)

    def last(cur):
        pv(qi - 1, p_bufs[1 - cur], a_bufs[1 - cur])
        sm(s_bufs[cur], p_bufs[cur], a_bufs[cur], True)
        pv_diagonal(qi, p_bufs[cur], a_bufs[cur])
        o_ref[...] = (acc_sc[:, pl.ds(0, HEAD_DIM)]
                      / acc_sc[:, pl.ds(HEAD_DIM, LANES)]).astype(o_ref.dtype)

    qk(0, s0)

    def pair(i, carry):
        half(2 * i, 0)
        half(2 * i + 1, 1)
        return carry

    lax.fori_loop(0, lax.shift_right_logical(qi, 1), pair, 0)
    odd = jnp.bitwise_and(qi, 1)

    @pl.when(odd == 1)
    def _():
        half(qi - 1, 0)
        last(1)

    @pl.when(odd == 0)
    def _():
        last(0)


def _flash(q, q1_blk, q2_blk, kv, k1_blk, v_blk, k2, heads, tb_pref=FLASH_BLOCK):
    s = q.shape[0]
    tb = _pick(s, tb_pref, LANES)
    rchunk = _pick(tb, 32, SUBLANES)
    assert tb % (2 * LANES) == 0 and (tb // 2) % rchunk == 0
    fox = q2_blk is None
    in_specs = [pl.BlockSpec((tb, HEAD_DIM), lambda h, i: (i, q1_blk + h))]
    args = [q]
    if not fox:
        in_specs.append(pl.BlockSpec((tb, HEAD_DIM), lambda h, i: (i, q2_blk + h)))
        args.append(q)
    in_specs += [pl.BlockSpec((s, HEAD_DIM), lambda h, i: (0, k1_blk + h)),
                 pl.BlockSpec((s, LANES), lambda h, i: (0, 0)),
                 pl.BlockSpec((s, HEAD_DIM), lambda h, i: (0, v_blk + h))]
    args += [kv, k2, kv]
    score = pltpu.VMEM((tb, tb), F32)
    prob = pltpu.VMEM((tb, tb), BF16)
    col = pltpu.VMEM((tb, 1), F32)
    return pl.pallas_call(
        functools.partial(_flash_body, tb=tb, fox=fox, rchunk=rchunk),
        grid=(heads, s // tb),
        in_specs=in_specs,
        out_specs=pl.BlockSpec((tb, HEAD_DIM), lambda h, i: (i, h)),
        out_shape=jax.ShapeDtypeStruct((s, heads * HEAD_DIM), BF16),
        scratch_shapes=[pltpu.VMEM((tb, HEAD_DIM + LANES), BF16), score, score, prob, prob,
                        col, col, col, pltpu.VMEM((tb, HEAD_DIM + LANES), F32)],
        compiler_params=_cparams("parallel", "arbitrary"),
        name="flash_fox" if fox else "flash_mla",
    )(*args)


def _pool_body(u_ref, halo_ref, w_ref, sc_ref, o_ref, buf, *, groups, gdim):
    i = pl.program_id(0)
    tm = u_ref.shape[0]
    buf[pl.ds(0, POOL_HALO), :] = jnp.where(i == 0, 0.0, halo_ref[...])
    buf[pl.ds(POOL_HALO, tm), :] = u_ref[...]
    t = (i * tm + lax.broadcasted_iota(jnp.int32, (tm, 1), 0)).astype(F32)
    for g in range(groups):
        win = POOL_WINDOWS[g]
        c0 = g * gdim
        tot = buf[pl.ds(POOL_HALO, tm), pl.ds(c0, gdim)]
        cur = tot
        for lag in range(1, win):
            tot = tot + buf[pl.ds(POOL_HALO - lag, tm), pl.ds(c0, gdim)]
        count = jnp.minimum(t + 1.0, float(win))
        pooled = tot / count - cur
        mixed = jnp.dot(pooled.astype(BF16), w_ref[g], preferred_element_type=F32)
        o_ref[:, pl.ds(c0, gdim)] = (mixed * sc_ref[:, pl.ds(c0, gdim)]).astype(o_ref.dtype)


def _pool(uf, pool_w_b, pool_scale):
    s = uf.shape[0]
    groups, gdim, _ = pool_w_b.shape
    c = groups * gdim
    tm = _pick(s, 256, POOL_HALO)
    hb = tm // POOL_HALO
    return pl.pallas_call(
        functools.partial(_pool_body, groups=groups, gdim=gdim),
        grid=(s // tm,),
        in_specs=[pl.BlockSpec((tm, c), lambda i: (i, 0)),
                  pl.BlockSpec((POOL_HALO, c), lambda i: (jnp.maximum(i * hb - 1, 0), 0)),
                  pl.BlockSpec((groups, gdim, gdim), lambda i: (0, 0, 0)),
                  pl.BlockSpec((1, c), lambda i: (0, 0))],
        out_specs=pl.BlockSpec((tm, c), lambda i: (i, 0)),
        out_shape=jax.ShapeDtypeStruct((s, c), BF16),
        scratch_shapes=[pltpu.VMEM((tm + POOL_HALO, c), F32)],
        compiler_params=_cparams("parallel"),
        name="multiscale_pool",
    )(uf, uf, pool_w_b, pool_scale.reshape(1, c))


def _conv_body(a_ref, g_ref, ha_ref, hg_ref, w_ref, cb_ref, lg_ref, lb_ref, o_ref, buf, ybuf, *, rows, lanes):
    i = pl.program_id(0)
    tm, c = a_ref.shape
    halo = ha_ref[...] * jax.nn.sigmoid(hg_ref[...])
    buf[pl.ds(0, CONV_HALO), :] = jnp.where(i == 0, 0.0, halo)
    buf[pl.ds(CONV_HALO, tm), :] = a_ref[...] * jax.nn.sigmoid(g_ref[...])
    base = CONV_HALO - (CONV_TAPS - 1)
    for r0 in range(0, tm, rows):
        for c0 in range(0, c, lanes):
            acc = jnp.zeros((rows, lanes), F32) + cb_ref[:, pl.ds(c0, lanes)]
            for j in range(CONV_TAPS):
                acc = acc + w_ref[pl.ds(j, 1), pl.ds(c0, lanes)] * buf[pl.ds(base + r0 + j, rows), pl.ds(c0, lanes)]
            ybuf[pl.ds(r0, rows), pl.ds(c0, lanes)] = acc
    y = ybuf[...]
    mu = jnp.mean(y, axis=-1, keepdims=True)
    yc = y - mu
    var = jnp.mean(yc * yc, axis=-1, keepdims=True)
    z = yc * lax.rsqrt(var + LN_EPS) * lg_ref[...] + lb_ref[...]
    o_ref[...] = (z * jax.nn.sigmoid(z)).astype(o_ref.dtype)


def _conformer_conv(h1, cw, conv_w_p, conv_b, ln_g, ln_b):
    s = h1.shape[0]
    tm = _pick(s, 128, CONV_HALO)
    hb = tm // CONV_HALO
    rows = _pick(tm, 32, SUBLANES)
    lanes = _pick(cw, 256, LANES)
    nblk = 1
    vec = lambda a: a.reshape(1, cw)
    return pl.pallas_call(
        functools.partial(_conv_body, rows=rows, lanes=lanes),
        grid=(s // tm,),
        in_specs=[pl.BlockSpec((tm, cw), lambda i: (i, 0)),
                  pl.BlockSpec((tm, cw), lambda i: (i, nblk)),
                  pl.BlockSpec((CONV_HALO, cw), lambda i: (jnp.maximum(i * hb - 1, 0), 0)),
                  pl.BlockSpec((CONV_HALO, cw), lambda i: (jnp.maximum(i * hb - 1, 0), nblk)),
                  pl.BlockSpec((CONV_HALO, cw), lambda i: (0, 0)),
                  pl.BlockSpec((1, cw), lambda i: (0, 0)),
                  pl.BlockSpec((1, cw), lambda i: (0, 0)),
                  pl.BlockSpec((1, cw), lambda i: (0, 0))],
        out_specs=pl.BlockSpec((tm, cw), lambda i: (i, 0)),
        out_shape=jax.ShapeDtypeStruct((s, cw), BF16),
        scratch_shapes=[pltpu.VMEM((tm + CONV_HALO, cw), F32), pltpu.VMEM((tm, cw), F32)],
        compiler_params=_cparams("parallel"),
        name="conformer_conv",
    )(h1, h1, h1, h1, conv_w_p, vec(conv_b), vec(ln_g), vec(ln_b))


def _rope128(r, cosv, sinv, lane):
    half = MLA_ROPE_DIM // 2
    s1 = jnp.where(lane < half, -sinv, 0.0)
    s2 = jnp.where((lane >= half) & (lane < 2 * half), sinv, 0.0)
    return r * cosv + pltpu.roll(r, LANES - half, 1) * s1 + pltpu.roll(r, half, 1) * s2


def _rope_k_body(pos_ref, invf_ref, kpe_ref, cos_ref, sin_ref, k2_ref):
    tm = pos_ref.shape[0]
    ang = pos_ref[...].astype(F32) * invf_ref[...]
    cosv = jnp.cos(ang)
    sinv = jnp.sin(ang)
    cos_ref[...] = cosv
    sin_ref[...] = sinv
    lane = lax.broadcasted_iota(jnp.int32, (tm, LANES), 1)
    k2_ref[...] = _rope128(kpe_ref[...], cosv, sinv, lane).astype(BF16)


def _rope_k(positions_col, invf, h1, kpe_block):
    s = h1.shape[0]
    tm = _pick(s, 512, SUBLANES)
    tab = jax.ShapeDtypeStruct((s, LANES), F32)
    return pl.pallas_call(
        _rope_k_body,
        grid=(s // tm,),
        in_specs=[pl.BlockSpec((tm, 1), lambda i: (i, 0)),
                  pl.BlockSpec((1, LANES), lambda i: (0, 0)),
                  pl.BlockSpec((tm, LANES), lambda i: (i, kpe_block))],
        out_specs=[pl.BlockSpec((tm, LANES), lambda i: (i, 0))] * 3,
        out_shape=[tab, tab, jax.ShapeDtypeStruct((s, LANES), BF16)],
        compiler_params=_cparams("parallel"),
        name="rope_tables_key",
    )(positions_col, invf, h1)


def _rms_proj_body(c_ref, g_ref, w_ref, cos_ref, sin_ref, o_ref, xn_ref, *, scale, rope_from):
    j = pl.program_id(1)
    tm, tn = o_ref.shape

    @pl.when(j == 0)
    def _():
        xf = c_ref[...]
        ms = jnp.mean(xf * xf, axis=-1, keepdims=True)
        xn_ref[...] = (xf * lax.rsqrt(ms + RMS_EPS) * g_ref[...]).astype(BF16)

    acc = jnp.dot(xn_ref[...], w_ref[...], preferred_element_type=F32)

    def plain():
        o_ref[...] = (acc * scale).astype(o_ref.dtype)

    if rope_from is None:
        plain()
        return

    pl.when(j < rope_from)(plain)

    @pl.when(j >= rope_from)
    def _():
        lane = lax.broadcasted_iota(jnp.int32, (tm, LANES), 1)
        cosv = cos_ref[...]
        sinv = sin_ref[...]
        for c0 in range(0, tn, LANES):
            r = _rope128(acc[:, c0:c0 + LANES], cosv, sinv, lane)
            o_ref[:, pl.ds(c0, LANES)] = (r * scale).astype(o_ref.dtype)


def _rms_proj(h1, cblock, rank, g, w, cosv, sinv, scale, rope_cols):
    s = h1.shape[0]
    n = w.shape[1]
    tm = _pick(s, 1024, SUBLANES)
    tn = _pick(n, 1024, LANES)
    if rope_cols:
        tn = math.gcd(tn, n - rope_cols)
    rope_from = (n - rope_cols) // tn if rope_cols else None
    return pl.pallas_call(
        functools.partial(_rms_proj_body, scale=scale, rope_from=rope_from),
        grid=(s // tm, n // tn),
        in_specs=[pl.BlockSpec((tm, rank), lambda i, j: (i, cblock)),
                  pl.BlockSpec((1, rank), lambda i, j: (0, 0)),
                  pl.BlockSpec((rank, tn), lambda i, j: (0, j)),
                  pl.BlockSpec((tm, LANES), lambda i, j: (i, 0)),
                  pl.BlockSpec((tm, LANES), lambda i, j: (i, 0))],
        out_specs=pl.BlockSpec((tm, tn), lambda i, j: (i, j)),
        out_shape=jax.ShapeDtypeStruct((s, n), BF16),
        scratch_shapes=[pltpu.VMEM((tm, rank), BF16)],
        compiler_params=_cparams("parallel", "arbitrary"),
        name="rms_up_proj",
    )(h1, g.reshape(1, rank), w, cosv, sinv)


def _ffn(x32, x16, w1, w3, w2, g, b, emit_bf16=True):
    d, f = w1.shape
    fp = -(-f // 512) * 512
    w2p = jnp.pad(w2.astype(BF16), ((0, fp - f), (0, 0)))
    hidden = _ffn_gateup(x16, w1.astype(BF16), w3.astype(BF16), fp)
    return _proj_ln((hidden,), w2p, x32, g, b, 0.5, emit_bf16)


def _mixer_fox_pool(x32, x16, w_in, b_f, pool_w, pool_scale, w_out, g, b):
    d = x32.shape[1]
    heads = b_f.shape[0]
    fw = heads * HEAD_DIM
    groups, gdim, _ = pool_w.shape
    pw = groups * gdim
    w_qkv = w_in[:, :3 * fw].astype(BF16)
    w_uf = jnp.concatenate([w_in[:, 3 * fw + heads:], w_in[:, 3 * fw:3 * fw + heads],
                            jnp.zeros((d, LANES - heads), F32)], axis=1).astype(BF16)
    b_f_pad = jnp.pad(b_f, (0, LANES - heads)).reshape(1, LANES)

    qkv = _proj(x16, w_qkv, BF16, 1024, scale=LOG2E * HEAD_DIM ** -0.5, scaled_cols=fw)
    uf = _proj(x16, w_uf, F32, pw + LANES)
    gate_parts = _fox_gates(uf, pw, b_f_pad, heads)
    y_a = _flash(qkv, 0, None, qkv, heads, 2 * heads, gate_parts, heads)
    y_b = _pool(uf, pool_w.astype(BF16), pool_scale)
    return _proj_ln((y_a, y_b), w_out.astype(BF16), x32, g, b, 1.0)


def _mixer_conv_mla(x32, x16, positions, w_in, conv_w, conv_b, conv_ln_g, conv_ln_b,
                    q_norm_g, w_uq, kv_norm_g, w_ukv, w_out, g, b):
    s, d = x32.shape
    cw = conv_b.shape[0]
    q_rank = q_norm_g.shape[0]
    kv_rank = kv_norm_g.shape[0]
    qk_dim = HEAD_DIM + MLA_ROPE_DIM
    heads = w_uq.shape[1] // qk_dim
    hw = heads * HEAD_DIM
    n_in = w_in.shape[1]
    n_in_p = -(-(n_in + MLA_ROPE_DIM) // 768) * 768
    w_in_p = jnp.pad(w_in.astype(BF16), ((0, 0), (0, n_in_p - n_in)))
    c_q_off = 2 * cw
    c_kv_off = c_q_off + q_rank
    kpe_off = c_kv_off + kv_rank
    assert c_q_off % q_rank == 0 and c_kv_off % kv_rank == 0 and kpe_off % LANES == 0

    wq = w_uq.reshape(q_rank, heads, qk_dim)
    wq_rope = jnp.pad(wq[:, :, HEAD_DIM:], ((0, 0), (0, 0), (0, LANES - MLA_ROPE_DIM)))
    wq_p = jnp.concatenate([wq[:, :, :HEAD_DIM].reshape(q_rank, hw),
                            wq_rope.reshape(q_rank, heads * LANES)], axis=1).astype(BF16)
    wkv = w_ukv.reshape(kv_rank, heads, 2 * HEAD_DIM)
    wkv_p = jnp.concatenate([wkv[:, :, :HEAD_DIM].reshape(kv_rank, hw),
                             wkv[:, :, HEAD_DIM:].reshape(kv_rank, hw)], axis=1).astype(BF16)
    conv_w_p = jnp.pad(conv_w.reshape(CONV_TAPS, cw), ((0, CONV_HALO - CONV_TAPS), (0, 0)))
    half = MLA_ROPE_DIM // 2
    invf = np.zeros((1, LANES), np.float32)
    freqs = ROPE_BASE ** (-np.arange(half, dtype=np.float32) / half)
    invf[0, :half] = freqs
    invf[0, half:2 * half] = freqs
    invf = jnp.asarray(invf)

    h1 = _proj(x16, w_in_p, F32, 768)
    y_c = _conformer_conv(h1, cw, conv_w_p, conv_b, conv_ln_g, conv_ln_b)
    cosv, sinv, k2 = _rope_k(positions.reshape(s, 1), invf, h1, kpe_off // LANES)
    q = _rms_proj(h1, c_q_off // q_rank, q_rank, q_norm_g, wq_p, cosv, sinv,
                  LOG2E * qk_dim ** -0.5, heads * LANES)
    kv = _rms_proj(h1, c_kv_off // kv_rank, kv_rank, kv_norm_g, wkv_p, cosv, sinv, 1.0, 0)
    y_d = _flash(q, 0, heads, kv, 0, heads, k2, heads)
    return _proj_ln((y_c, y_d), w_out.astype(BF16), x32, g, b, 1.0)


def kernel(x, positions, l0_ffn1_w1, l0_ffn1_w3, l0_ffn1_w2, l0_ln_ffn1_g, l0_ln_ffn1_b, l0_w_in, l0_b_f, l0_pool_w, l0_pool_scale, l0_w_out, l0_ln_mix_g, l0_ln_mix_b, l0_ffn2_w1, l0_ffn2_w3, l0_ffn2_w2, l0_ln_ffn2_g, l0_ln_ffn2_b, l1_ffn1_w1, l1_ffn1_w3, l1_ffn1_w2, l1_ln_ffn1_g, l1_ln_ffn1_b, l1_w_in, l1_conv_w, l1_conv_b, l1_conv_ln_g, l1_conv_ln_b, l1_q_norm_g, l1_w_uq, l1_kv_norm_g, l1_w_ukv, l1_w_out, l1_ln_mix_g, l1_ln_mix_b, l1_ffn2_w1, l1_ffn2_w3, l1_ffn2_w2, l1_ln_ffn2_g, l1_ln_ffn2_b):
    bsz, s, d = x.shape
    outs = []
    for bi in range(bsz):
        x32 = x.reshape(s, d) if bsz == 1 else x[bi]
        x16 = x32.astype(BF16)
        pos = positions.reshape(s) if bsz == 1 else positions[bi]
        x32, x16 = _ffn(x32, x16, l0_ffn1_w1, l0_ffn1_w3, l0_ffn1_w2, l0_ln_ffn1_g, l0_ln_ffn1_b)
        x32, x16 = _mixer_fox_pool(x32, x16, l0_w_in, l0_b_f, l0_pool_w, l0_pool_scale, l0_w_out,
                                   l0_ln_mix_g, l0_ln_mix_b)
        x32, x16 = _ffn(x32, x16, l0_ffn2_w1, l0_ffn2_w3, l0_ffn2_w2, l0_ln_ffn2_g, l0_ln_ffn2_b)
        x32, x16 = _ffn(x32, x16, l1_ffn1_w1, l1_ffn1_w3, l1_ffn1_w2, l1_ln_ffn1_g, l1_ln_ffn1_b)
        x32, x16 = _mixer_conv_mla(x32, x16, pos, l1_w_in, l1_conv_w, l1_conv_b, l1_conv_ln_g,
                                   l1_conv_ln_b, l1_q_norm_g, l1_w_uq, l1_kv_norm_g, l1_w_ukv,
                                   l1_w_out, l1_ln_mix_g, l1_ln_mix_b)
        x32, _ = _ffn(x32, x16, l1_ffn2_w1, l1_ffn2_w3, l1_ffn2_w2, l1_ln_ffn2_g, l1_ln_ffn2_b,
                      emit_bf16=False)
        outs.append(x32)
    return outs[0].reshape(1, s, d) if bsz == 1 else jnp.stack(outs, axis=0)
```

```python
import functools
import math

import numpy as np
import jax
import jax.numpy as jnp
from jax import lax
from jax.experimental import pallas as pl
from jax.experimental.pallas import tpu as pltpu

F32 = jnp.float32
BF16 = jnp.bfloat16

DEPTH = 2
ALPHA = (2 * DEPTH) ** 0.25
LN_EPS = 1e-5
RMS_EPS = 1e-6
HEAD_DIM = 128
MLA_ROPE_DIM = 64
ROPE_BASE = 10000.0
POOL_WINDOWS = (2, 4, 8, 16)
CONV_TAPS = 31

V7X_VMEM_LIMIT_BYTES = 56 * 1024 * 1024
LANES = 128
SUBLANES = 8
POOL_HALO = 16
CONV_HALO = 32
GATE_LANE_STRIDE = 32
FLASH_BLOCK = 1024
LN_ROWS = 128
LOG2E = math.log2(math.e)


def _pick(n, pref, align):
    best = None
    for d in range(align, min(n, pref) + 1, align):
        if n % d == 0:
            best = d
    return n if best is None else best


def _cparams(*sem):
    return pltpu.CompilerParams(dimension_semantics=sem, vmem_limit_bytes=V7X_VMEM_LIMIT_BYTES)


def _split3(x):
    hi = x.astype(BF16)
    r1 = x - hi.astype(F32)
    mid = r1.astype(BF16)
    lo = (r1 - mid.astype(F32)).astype(BF16)
    return hi, mid, lo


def _cast_body(x_ref, o_ref):
    o_ref[...] = x_ref[...].astype(o_ref.dtype)


def _to_bf16(x):
    s, d = x.shape
    tm = _pick(s, 512, SUBLANES)
    return pl.pallas_call(
        _cast_body,
        grid=(s // tm,),
        in_specs=[pl.BlockSpec((tm, d), lambda i: (i, 0))],
        out_specs=pl.BlockSpec((tm, d), lambda i: (i, 0)),
        out_shape=jax.ShapeDtypeStruct((s, d), BF16),
        compiler_params=_cparams("parallel"),
        name="cast_bf16",
    )(x)


def _gateup_body(x_ref, w1_ref, w3_ref, o_ref, *, f):
    tm, tn = o_ref.shape
    x = x_ref[...]
    a = jnp.dot(x, w1_ref[...], preferred_element_type=F32)
    b = jnp.dot(x, w3_ref[...], preferred_element_type=F32)
    hidden = a * jax.nn.sigmoid(a) * b
    col = pl.program_id(1) * tn + lax.broadcasted_iota(jnp.int32, (tm, tn), 1)
    o_ref[...] = jnp.where(col < f, hidden, 0.0).astype(o_ref.dtype)


def _ffn_gateup(xb, w1, w3, fp):
    s, d = xb.shape
    f = w1.shape[1]
    tm = _pick(s, 1024, SUBLANES)
    tn = _pick(fp, 512, LANES)
    return pl.pallas_call(
        functools.partial(_gateup_body, f=f),
        grid=(s // tm, fp // tn),
        in_specs=[pl.BlockSpec((tm, d), lambda i, j: (i, 0)),
                  pl.BlockSpec((d, tn), lambda i, j: (0, j)),
                  pl.BlockSpec((d, tn), lambda i, j: (0, j))],
        out_specs=pl.BlockSpec((tm, tn), lambda i, j: (i, j)),
        out_shape=jax.ShapeDtypeStruct((s, fp), BF16),
        compiler_params=_cparams("parallel", "arbitrary"),
        name="ffn_gateup",
    )(xb, w1, w3)


def _proj_ln_body(*refs, r, nk, nk_first, rows, ncols, emit_bf16):
    two_lhs = nk_first < nk
    ya_ref = refs[0]
    yb_ref = refs[1] if two_lhs else None
    w_ref, x_ref, g_ref, b_ref, o32_ref = refs[1 + two_lhs:6 + two_lhs]
    o16_ref = refs[6 + two_lhs] if emit_bf16 else None
    k = pl.program_id(1)

    @pl.when(k == 0)
    def _():
        o32_ref[...] = jnp.zeros(o32_ref.shape, F32)

    y = ya_ref[...]
    if two_lhs:
        y = jnp.where(k < nk_first, y, yb_ref[...])
    d = o32_ref.shape[1]
    for c0 in range(0, d, ncols):
        o32_ref[:, pl.ds(c0, ncols)] += jnp.dot(y, w_ref[:, pl.ds(c0, ncols)],
                                                preferred_element_type=F32)

    @pl.when(k == nk - 1)
    def _():
        g = g_ref[...]
        b = b_ref[...]
        tm = o32_ref.shape[0]

        def chunk(c, carry):
            r0 = pl.multiple_of(c * rows, rows)
            z = ALPHA * x_ref[pl.ds(r0, rows), :] + r * o32_ref[pl.ds(r0, rows), :]
            mu = jnp.mean(z, axis=-1, keepdims=True)
            zc = z - mu
            var = jnp.mean(zc * zc, axis=-1, keepdims=True)
            out = zc * lax.rsqrt(var + LN_EPS) * g + b
            o32_ref[pl.ds(r0, rows), :] = out
            if emit_bf16:
                o16_ref[pl.ds(r0, rows), :] = out.astype(BF16)
            return carry

        lax.fori_loop(0, tm // rows, chunk, 0)


def _proj_ln(ys, w, x, g, b, r, emit_bf16=True):
    s = ys[0].shape[0]
    kdim, d = w.shape
    tm = _pick(s, 512, SUBLANES)
    tk = _pick(math.gcd(*[y.shape[1] for y in ys]), 512, LANES)
    nk = kdim // tk
    nk_first = ys[0].shape[1] // tk
    y_specs = [pl.BlockSpec((tm, tk), lambda i, k: (i, jnp.minimum(k, nk_first - 1)))]
    if len(ys) == 2:
        y_specs.append(pl.BlockSpec((tm, tk), lambda i, k: (i, jnp.maximum(k - nk_first, 0))))
    rows = _pick(tm, LN_ROWS, SUBLANES)
    ncols = _pick(d, 1024, LANES)
    out_shape = [jax.ShapeDtypeStruct((s, d), F32)]
    out_specs = [pl.BlockSpec((tm, d), lambda i, k: (i, 0))]
    if emit_bf16:
        out_shape.append(jax.ShapeDtypeStruct((s, d), BF16))
        out_specs.append(pl.BlockSpec((tm, d), lambda i, k: (i, 0)))
    outs = pl.pallas_call(
        functools.partial(_proj_ln_body, r=r, nk=nk, nk_first=nk_first, rows=rows, ncols=ncols,
                          emit_bf16=emit_bf16),
        grid=(s // tm, nk),
        in_specs=y_specs + [pl.BlockSpec((tk, d), lambda i, k: (k, 0)),
                            pl.BlockSpec((tm, d), lambda i, k: (i, 0)),
                            pl.BlockSpec((1, d), lambda i, k: (0, 0)),
                            pl.BlockSpec((1, d), lambda i, k: (0, 0))],
        out_specs=out_specs,
        out_shape=out_shape,
        compiler_params=_cparams("parallel", "arbitrary"),
        name="proj_residual_ln",
    )(*ys, w, x, g.reshape(1, d), b.reshape(1, d))
    return outs if emit_bf16 else (outs[0], None)


def _proj_body(x_ref, w_ref, o_ref, *, scale, scaled_blocks):
    acc = jnp.dot(x_ref[...], w_ref[...], preferred_element_type=F32)
    if scaled_blocks:
        acc = acc * jnp.where(pl.program_id(1) < scaled_blocks, scale, 1.0).astype(F32)
    o_ref[...] = acc.astype(o_ref.dtype)


def _proj(xb, w, out_dtype, tn_pref, scale=1.0, scaled_cols=0):
    s, d = xb.shape
    n = w.shape[1]
    tm = _pick(s, 1024, SUBLANES)
    tn = _pick(n, tn_pref, LANES)
    if scaled_cols:
        tn = math.gcd(tn, scaled_cols)
    return pl.pallas_call(
        functools.partial(_proj_body, scale=scale, scaled_blocks=scaled_cols // tn),
        grid=(s // tm, n // tn),
        in_specs=[pl.BlockSpec((tm, d), lambda i, j: (i, 0)),
                  pl.BlockSpec((d, tn), lambda i, j: (0, j))],
        out_specs=pl.BlockSpec((tm, tn), lambda i, j: (i, j)),
        out_shape=jax.ShapeDtypeStruct((s, n), out_dtype),
        compiler_params=_cparams("parallel", "arbitrary"),
        name="proj",
    )(xb, w)


def _gates_body(f_ref, b_ref, o_ref, carry_ref, *, heads):
    i = pl.program_id(0)
    tm = f_ref.shape[0]

    @pl.when(i == 0)
    def _():
        carry_ref[...] = jnp.zeros_like(carry_ref)

    lane = lax.broadcasted_iota(jnp.int32, (tm, LANES), 1)
    xv = f_ref[...] + b_ref[...]
    logf = jnp.minimum(xv, 0.0) - jnp.log1p(jnp.exp(-jnp.abs(xv)))
    logf = jnp.where(lane < heads, logf, 0.0)
    row = lax.broadcasted_iota(jnp.int32, (tm, tm), 0)
    col = lax.broadcasted_iota(jnp.int32, (tm, tm), 1)
    tril = jnp.where(row >= col, 1.0, 0.0).astype(BF16)
    hi, mid, lo = _split3(logf)
    c = (jnp.dot(tril, hi, preferred_element_type=F32)
         + jnp.dot(tril, mid, preferred_element_type=F32)
         + jnp.dot(tril, lo, preferred_element_type=F32)) + carry_ref[...]
    carry_ref[...] = c[tm - 1:tm, :]
    nhi, nmid, nlo = _split3(-LOG2E * c)
    packed = jnp.where(lane < GATE_LANE_STRIDE, nhi.astype(F32),
                       jnp.where(lane < 2 * GATE_LANE_STRIDE,
                                 pltpu.roll(nmid.astype(F32), GATE_LANE_STRIDE, 1),
                                 pltpu.roll(nlo.astype(F32), 2 * GATE_LANE_STRIDE, 1)))
    o_ref[...] = packed.astype(BF16)


def _fox_gates(uf, ucols, b_f_pad, heads):
    s = uf.shape[0]
    tm = _pick(s, 512, SUBLANES)
    fblk = ucols // LANES
    return pl.pallas_call(
        functools.partial(_gates_body, heads=heads),
        grid=(s // tm,),
        in_specs=[pl.BlockSpec((tm, LANES), lambda i: (i, fblk)),
                  pl.BlockSpec((1, LANES), lambda i: (0, 0))],
        out_specs=pl.BlockSpec((tm, LANES), lambda i: (i, 0)),
        out_shape=jax.ShapeDtypeStruct((s, LANES), BF16),
        scratch_shapes=[pltpu.VMEM((1, LANES), F32)],
        compiler_params=_cparams("arbitrary"),
        name="fox_gates",
    )(uf, b_f_pad)


def _flash_body(*refs, tb, fox, rchunk):
    n_in = 4 if fox else 5
    q1_ref = refs[0]
    q2_ref = None if fox else refs[1]
    k1_ref, k2_ref, v_ref = refs[n_in - 3:n_in]
    o_ref = refs[n_in]
    q_sc, s0, s1, p0, p1, a0, a1, m_sc, acc_sc = refs[n_in + 1:]
    s_bufs, p_bufs, a_bufs = (s0, s1), (p0, p1), (a0, a1)
    h = pl.program_id(0)
    qi = pl.program_id(1)

    q_sc[:, pl.ds(0, HEAD_DIM)] = q1_ref[...]
    if fox:
        lane = lax.broadcasted_iota(jnp.int32, (tb, LANES), 1)
        sel = (lane < 3 * GATE_LANE_STRIDE) & (jnp.bitwise_and(lane, GATE_LANE_STRIDE - 1) == h)
        q_sc[:, pl.ds(HEAD_DIM, LANES)] = jnp.where(sel, 1.0, 0.0).astype(BF16)
    else:
        q_sc[:, pl.ds(HEAD_DIM, LANES)] = q2_ref[...]
    m_sc[...] = jnp.full(m_sc.shape, -jnp.inf, F32)
    acc_sc[...] = jnp.zeros(acc_sc.shape, F32)
    p1[...] = jnp.zeros(p1.shape, BF16)
    a1[...] = jnp.ones(a1.shape, F32)
    ones = jnp.ones((tb, LANES), BF16)
    hb = tb // 2

    def qk(b, s_dst):
        k0 = pl.multiple_of(b * tb, tb)
        kk = jnp.concatenate([k1_ref[pl.ds(k0, tb), :], k2_ref[pl.ds(k0, tb), :]], axis=1)
        s_dst[...] = lax.dot_general(q_sc[...], kk, (((1,), (1,)), ((), ())),
                                     preferred_element_type=F32)

    def pv(b, p_src, a_src):
        k0 = pl.multiple_of(jnp.maximum(b, 0) * tb, tb)
        vv = jnp.concatenate([v_ref[pl.ds(k0, tb), :], ones], axis=1)
        acc_sc[...] = a_src[...] * acc_sc[...] + jnp.dot(p_src[...], vv, preferred_element_type=F32)

    def sm(s_src, p_dst, a_dst, diagonal):
        for r0 in range(0, tb, rchunk):
            ncol = hb if diagonal and r0 + rchunk <= hb else tb
            s = s_src[pl.ds(r0, rchunk), pl.ds(0, ncol)]
            if diagonal:
                row = r0 + lax.broadcasted_iota(jnp.int32, (rchunk, ncol), 0)
                col = lax.broadcasted_iota(jnp.int32, (rchunk, ncol), 1)
                s = jnp.where(row >= col, s, -jnp.inf)
            m_prev = m_sc[pl.ds(r0, rchunk), :]
            m_new = jnp.maximum(m_prev, jnp.max(s, axis=1, keepdims=True))
            a_dst[pl.ds(r0, rchunk), :] = jnp.exp2(m_prev - m_new)
            p_dst[pl.ds(r0, rchunk), pl.ds(0, ncol)] = jnp.exp2(s - m_new).astype(BF16)
            m_sc[pl.ds(r0, rchunk), :] = m_new

    def pv_diagonal(b, p_src, a_src):
        k0 = pl.multiple_of(b * tb, tb)
        vv = jnp.concatenate([v_ref[pl.ds(k0, tb), :], ones], axis=1)
        acc_sc[pl.ds(0, hb), :] = (a_src[pl.ds(0, hb), :] * acc_sc[pl.ds(0, hb), :]
                                   + jnp.dot(p_src[pl.ds(0, hb), pl.ds(0, hb)], vv[:hb],
                                             preferred_element_type=F32))
        acc_sc[pl.ds(hb, hb), :] = (a_src[pl.ds(hb, hb), :] * acc_sc[pl.ds(hb, hb), :]
                                    + jnp.dot(p_src[pl.ds(hb, hb), :], vv,
                                              preferred_element_type=F32))

    def half(b, cur):
        nxt = 1 - cur
        pv(b - 1, p_bufs[nxt], a_bufs[nxt])
        qk(b + 1, s_bufs[nxt])
        sm(s_bufs[cur], p_bufs[cur], a_bufs[cur], False)

    def last(cur):
        pv(qi - 1, p_bufs[1 - cur], a_bufs[1 - cur])
        sm(s_bufs[cur], p_bufs[cur], a_bufs[cur], True)
        pv_diagonal(qi, p_bufs[cur], a_bufs[cur])
        o_ref[...] = (acc_sc[:, pl.ds(0, HEAD_DIM)]
                      / acc_sc[:, pl.ds(HEAD_DIM, LANES)]).astype(o_ref.dtype)

    qk(0, s0)

    def pair(i, carry):
        half(2 * i, 0)
        half(2 * i + 1, 1)
        return carry

    lax.fori_loop(0, lax.shift_right_logical(qi, 1), pair, 0)
    odd = jnp.bitwise_and(qi, 1)

    @pl.when(odd == 1)
    def _():
        half(qi - 1, 0)
        last(1)

    @pl.when(odd == 0)
    def _():
        last(0)


def _flash(q, q1_blk, q2_blk, kv, k1_blk, v_blk, k2, heads, tb_pref=FLASH_BLOCK):
    s = q.shape[0]
    tb = _pick(s, tb_pref, LANES)
    rchunk = _pick(tb, 32, SUBLANES)
    assert tb % (2 * LANES) == 0 and (tb // 2) % rchunk == 0
    fox = q2_blk is None
    in_specs = [pl.BlockSpec((tb, HEAD_DIM), lambda h, i: (i, q1_blk + h))]
    args = [q]
    if not fox:
        in_specs.append(pl.BlockSpec((tb, HEAD_DIM), lambda h, i: (i, q2_blk + h)))
        args.append(q)
    in_specs += [pl.BlockSpec((s, HEAD_DIM), lambda h, i: (0, k1_blk + h)),
                 pl.BlockSpec((s, LANES), lambda h, i: (0, 0)),
                 pl.BlockSpec((s, HEAD_DIM), lambda h, i: (0, v_blk + h))]
    args += [kv, k2, kv]
    score = pltpu.VMEM((tb, tb), F32)
    prob = pltpu.VMEM((tb, tb), BF16)
    col = pltpu.VMEM((tb, 1), F32)
    return pl.pallas_call(
        functools.partial(_flash_body, tb=tb, fox=fox, rchunk=rchunk),
        grid=(heads, s // tb),
        in_specs=in_specs,
        out_specs=pl.BlockSpec((tb, HEAD_DIM), lambda h, i: (i, h)),
        out_shape=jax.ShapeDtypeStruct((s, heads * HEAD_DIM), BF16),
        scratch_shapes=[pltpu.VMEM((tb, HEAD_DIM + LANES), BF16), score, score, prob, prob,
                        col, col, col, pltpu.VMEM((tb, HEAD_DIM + LANES), F32)],
        compiler_params=_cparams("parallel", "arbitrary"),
        name="flash_fox" if fox else "flash_mla",
    )(*args)


def _pool_body(u_ref, halo_ref, w_ref, sc_ref, o_ref, buf, *, groups, gdim):
    i = pl.program_id(0)
    tm = u_ref.shape[0]
    buf[pl.ds(0, POOL_HALO), :] = jnp.where(i == 0, 0.0, halo_ref[...])
    buf[pl.ds(POOL_HALO, tm), :] = u_ref[...]
    t = (i * tm + lax.broadcasted_iota(jnp.int32, (tm, 1), 0)).astype(F32)
    for g in range(groups):
        win = POOL_WINDOWS[g]
        c0 = g * gdim
        tot = buf[pl.ds(POOL_HALO, tm), pl.ds(c0, gdim)]
        cur = tot
        for lag in range(1, win):
            tot = tot + buf[pl.ds(POOL_HALO - lag, tm), pl.ds(c0, gdim)]
        count = jnp.minimum(t + 1.0, float(win))
        pooled = tot / count - cur
        mixed = jnp.dot(pooled.astype(BF16), w_ref[g], preferred_element_type=F32)
        o_ref[:, pl.ds(c0, gdim)] = (mixed * sc_ref[:, pl.ds(c0, gdim)]).astype(o_ref.dtype)


def _pool(uf, pool_w_b, pool_scale):
    s = uf.shape[0]
    groups, gdim, _ = pool_w_b.shape
    c = groups * gdim
    tm = _pick(s, 256, POOL_HALO)
    hb = tm // POOL_HALO
    return pl.pallas_call(
        functools.partial(_pool_body, groups=groups, gdim=gdim),
        grid=(s // tm,),
        in_specs=[pl.BlockSpec((tm, c), lambda i: (i, 0)),
                  pl.BlockSpec((POOL_HALO, c), lambda i: (jnp.maximum(i * hb - 1, 0), 0)),
                  pl.BlockSpec((groups, gdim, gdim), lambda i: (0, 0, 0)),
                  pl.BlockSpec((1, c), lambda i: (0, 0))],
        out_specs=pl.BlockSpec((tm, c), lambda i: (i, 0)),
        out_shape=jax.ShapeDtypeStruct((s, c), BF16),
        scratch_shapes=[pltpu.VMEM((tm + POOL_HALO, c), F32)],
        compiler_params=_cparams("parallel"),
        name="multiscale_pool",
    )(uf, uf, pool_w_b, pool_scale.reshape(1, c))


def _conv_body(a_ref, g_ref, ha_ref, hg_ref, w_ref, cb_ref, lg_ref, lb_ref, o_ref, buf, sh, ybuf, *, rows, lanes):
    i = pl.program_id(0)
    tm, c = a_ref.shape
    halo = ha_ref[...] * jax.nn.sigmoid(hg_ref[...])
    buf[pl.ds(0, CONV_HALO), :] = jnp.where(i == 0, 0.0, halo)
    buf[pl.ds(CONV_HALO, tm), :] = a_ref[...] * jax.nn.sigmoid(g_ref[...])
    span = sh.shape[1]
    for b in range(1, SUBLANES):
        sh[b - 1] = buf[pl.ds(b, span), :]
    base = CONV_HALO - (CONV_TAPS - 1)
    for r0 in range(0, tm, rows):
        for c0 in range(0, c, lanes):
            acc = jnp.zeros((rows, lanes), F32) + cb_ref[:, pl.ds(c0, lanes)]
            for j in range(CONV_TAPS):
                hi, lo = divmod(base + j, SUBLANES)
                if lo == 0:
                    win = buf[pl.ds(r0 + SUBLANES * hi, rows), pl.ds(c0, lanes)]
                else:
                    win = sh[lo - 1, pl.ds(r0 + SUBLANES * hi, rows), pl.ds(c0, lanes)]
                acc = acc + w_ref[pl.ds(j, 1), pl.ds(c0, lanes)] * win
            ybuf[pl.ds(r0, rows), pl.ds(c0, lanes)] = acc
    y = ybuf[...]
    mu = jnp.mean(y, axis=-1, keepdims=True)
    yc = y - mu
    var = jnp.mean(yc * yc, axis=-1, keepdims=True)
    z = yc * lax.rsqrt(var + LN_EPS) * lg_ref[...] + lb_ref[...]
    o_ref[...] = (z * jax.nn.sigmoid(z)).astype(o_ref.dtype)


def _conformer_conv(h1, cw, conv_w_p, conv_b, ln_g, ln_b):
    s = h1.shape[0]
    tm = _pick(s, 128, CONV_HALO)
    hb = tm // CONV_HALO
    rows = _pick(tm, 32, SUBLANES)
    lanes = _pick(cw, 256, LANES)
    nblk = 1
    vec = lambda a: a.reshape(1, cw)
    return pl.pallas_call(
        functools.partial(_conv_body, rows=rows, lanes=lanes),
        grid=(s // tm,),
        in_specs=[pl.BlockSpec((tm, cw), lambda i: (i, 0)),
                  pl.BlockSpec((tm, cw), lambda i: (i, nblk)),
                  pl.BlockSpec((CONV_HALO, cw), lambda i: (jnp.maximum(i * hb - 1, 0), 0)),
                  pl.BlockSpec((CONV_HALO, cw), lambda i: (jnp.maximum(i * hb - 1, 0), nblk)),
                  pl.BlockSpec((CONV_HALO, cw), lambda i: (0, 0)),
                  pl.BlockSpec((1, cw), lambda i: (0, 0)),
                  pl.BlockSpec((1, cw), lambda i: (0, 0)),
                  pl.BlockSpec((1, cw), lambda i: (0, 0))],
        out_specs=pl.BlockSpec((tm, cw), lambda i: (i, 0)),
        out_shape=jax.ShapeDtypeStruct((s, cw), BF16),
        scratch_shapes=[pltpu.VMEM((tm + CONV_HALO, cw), F32),
                        pltpu.VMEM((SUBLANES - 1, tm + CONV_HALO - SUBLANES, cw), F32),
                        pltpu.VMEM((tm, cw), F32)],
        compiler_params=_cparams("parallel"),
        name="conformer_conv",
    )(h1, h1, h1, h1, conv_w_p, vec(conv_b), vec(ln_g), vec(ln_b))


def _rope128(r, cosv, sinv, lane):
    half = MLA_ROPE_DIM // 2
    s1 = jnp.where(lane < half, -sinv, 0.0)
    s2 = jnp.where((lane >= half) & (lane < 2 * half), sinv, 0.0)
    return r * cosv + pltpu.roll(r, LANES - half, 1) * s1 + pltpu.roll(r, half, 1) * s2


def _rope_k_body(pos_ref, invf_ref, kpe_ref, cos_ref, sin_ref, k2_ref):
    tm = pos_ref.shape[0]
    ang = pos_ref[...].astype(F32) * invf_ref[...]
    cosv = jnp.cos(ang)
    sinv = jnp.sin(ang)
    cos_ref[...] = cosv
    sin_ref[...] = sinv
    lane = lax.broadcasted_iota(jnp.int32, (tm, LANES), 1)
    k2_ref[...] = _rope128(kpe_ref[...], cosv, sinv, lane).astype(BF16)


def _rope_k(positions_col, invf, h1, kpe_block):
    s = h1.shape[0]
    tm = _pick(s, 512, SUBLANES)
    tab = jax.ShapeDtypeStruct((s, LANES), F32)
    return pl.pallas_call(
        _rope_k_body,
        grid=(s // tm,),
        in_specs=[pl.BlockSpec((tm, 1), lambda i: (i, 0)),
                  pl.BlockSpec((1, LANES), lambda i: (0, 0)),
                  pl.BlockSpec((tm, LANES), lambda i: (i, kpe_block))],
        out_specs=[pl.BlockSpec((tm, LANES), lambda i: (i, 0))] * 3,
        out_shape=[tab, tab, jax.ShapeDtypeStruct((s, LANES), BF16)],
        compiler_params=_cparams("parallel"),
        name="rope_tables_key",
    )(positions_col, invf, h1)


def _rms_proj_body(c_ref, g_ref, w_ref, cos_ref, sin_ref, o_ref, xn_ref, *, scale, rope_from):
    j = pl.program_id(1)
    tm, tn = o_ref.shape

    @pl.when(j == 0)
    def _():
        xf = c_ref[...]
        ms = jnp.mean(xf * xf, axis=-1, keepdims=True)
        xn_ref[...] = (xf * lax.rsqrt(ms + RMS_EPS) * g_ref[...]).astype(BF16)

    acc = jnp.dot(xn_ref[...], w_ref[...], preferred_element_type=F32)

    def plain():
        o_ref[...] = (acc * scale).astype(o_ref.dtype)

    if rope_from is None:
        plain()
        return

    pl.when(j < rope_from)(plain)

    @pl.when(j >= rope_from)
    def _():
        lane = lax.broadcasted_iota(jnp.int32, (tm, LANES), 1)
        cosv = cos_ref[...]
        sinv = sin_ref[...]
        for c0 in range(0, tn, LANES):
            r = _rope128(acc[:, c0:c0 + LANES], cosv, sinv, lane)
            o_ref[:, pl.ds(c0, LANES)] = (r * scale).astype(o_ref.dtype)


def _rms_proj(h1, cblock, rank, g, w, cosv, sinv, scale, rope_cols):
    s = h1.shape[0]
    n = w.shape[1]
    tm = _pick(s, 1024, SUBLANES)
    tn = _pick(n, 1024, LANES)
    if rope_cols:
        tn = math.gcd(tn, n - rope_cols)
    rope_from = (n - rope_cols) // tn if rope_cols else None
    return pl.pallas_call(
        functools.partial(_rms_proj_body, scale=scale, rope_from=rope_from),
        grid=(s // tm, n // tn),
        in_specs=[pl.BlockSpec((tm, rank), lambda i, j: (i, cblock)),
                  pl.BlockSpec((1, rank), lambda i, j: (0, 0)),
                  pl.BlockSpec((rank, tn), lambda i, j: (0, j)),
                  pl.BlockSpec((tm, LANES), lambda i, j: (i, 0)),
                  pl.BlockSpec((tm, LANES), lambda i, j: (i, 0))],
        out_specs=pl.BlockSpec((tm, tn), lambda i, j: (i, j)),
        out_shape=jax.ShapeDtypeStruct((s, n), BF16),
        scratch_shapes=[pltpu.VMEM((tm, rank), BF16)],
        compiler_params=_cparams("parallel", "arbitrary"),
        name="rms_up_proj",
    )(h1, g.reshape(1, rank), w, cosv, sinv)


def _ffn(x32, x16, w1, w3, w2, g, b, emit_bf16=True):
    d, f = w1.shape
    fp = -(-f // 512) * 512
    w2p = jnp.pad(w2.astype(BF16), ((0, fp - f), (0, 0)))
    hidden = _ffn_gateup(x16, w1.astype(BF16), w3.astype(BF16), fp)
    return _proj_ln((hidden,), w2p, x32, g, b, 0.5, emit_bf16)


def _mixer_fox_pool(x32, x16, w_in, b_f, pool_w, pool_scale, w_out, g, b):
    d = x32.shape[1]
    heads = b_f.shape[0]
    fw = heads * HEAD_DIM
    groups, gdim, _ = pool_w.shape
    pw = groups * gdim
    w_qkv = w_in[:, :3 * fw].astype(BF16)
    w_uf = jnp.concatenate([w_in[:, 3 * fw + heads:], w_in[:, 3 * fw:3 * fw + heads],
                            jnp.zeros((d, LANES - heads), F32)], axis=1).astype(BF16)
    b_f_pad = jnp.pad(b_f, (0, LANES - heads)).reshape(1, LANES)

    qkv = _proj(x16, w_qkv, BF16, 1024, scale=LOG2E * HEAD_DIM ** -0.5, scaled_cols=fw)
    uf = _proj(x16, w_uf, F32, pw + LANES)
    gate_parts = _fox_gates(uf, pw, b_f_pad, heads)
    y_a = _flash(qkv, 0, None, qkv, heads, 2 * heads, gate_parts, heads)
    y_b = _pool(uf, pool_w.astype(BF16), pool_scale)
    return _proj_ln((y_a, y_b), w_out.astype(BF16), x32, g, b, 1.0)


def _mixer_conv_mla(x32, x16, positions, w_in, conv_w, conv_b, conv_ln_g, conv_ln_b,
                    q_norm_g, w_uq, kv_norm_g, w_ukv, w_out, g, b):
    s, d = x32.shape
    cw = conv_b.shape[0]
    q_rank = q_norm_g.shape[0]
    kv_rank = kv_norm_g.shape[0]
    qk_dim = HEAD_DIM + MLA_ROPE_DIM
    heads = w_uq.shape[1] // qk_dim
    hw = heads * HEAD_DIM
    n_in = w_in.shape[1]
    n_in_p = -(-(n_in + MLA_ROPE_DIM) // 768) * 768
    w_in_p = jnp.pad(w_in.astype(BF16), ((0, 0), (0, n_in_p - n_in)))
    c_q_off = 2 * cw
    c_kv_off = c_q_off + q_rank
    kpe_off = c_kv_off + kv_rank
    assert c_q_off % q_rank == 0 and c_kv_off % kv_rank == 0 and kpe_off % LANES == 0

    wq = w_uq.reshape(q_rank, heads, qk_dim)
    wq_rope = jnp.pad(wq[:, :, HEAD_DIM:], ((0, 0), (0, 0), (0, LANES - MLA_ROPE_DIM)))
    wq_p = jnp.concatenate([wq[:, :, :HEAD_DIM].reshape(q_rank, hw),
                            wq_rope.reshape(q_rank, heads * LANES)], axis=1).astype(BF16)
    wkv = w_ukv.reshape(kv_rank, heads, 2 * HEAD_DIM)
    wkv_p = jnp.concatenate([wkv[:, :, :HEAD_DIM].reshape(kv_rank, hw),
                             wkv[:, :, HEAD_DIM:].reshape(kv_rank, hw)], axis=1).astype(BF16)
    conv_w_p = jnp.pad(conv_w.reshape(CONV_TAPS, cw), ((0, CONV_HALO - CONV_TAPS), (0, 0)))
    half = MLA_ROPE_DIM // 2
    invf = np.zeros((1, LANES), np.float32)
    freqs = ROPE_BASE ** (-np.arange(half, dtype=np.float32) / half)
    invf[0, :half] = freqs
    invf[0, half:2 * half] = freqs
    invf = jnp.asarray(invf)

    h1 = _proj(x16, w_in_p, F32, 768)
    y_c = _conformer_conv(h1, cw, conv_w_p, conv_b, conv_ln_g, conv_ln_b)
    cosv, sinv, k2 = _rope_k(positions.reshape(s, 1), invf, h1, kpe_off // LANES)
    q = _rms_proj(h1, c_q_off // q_rank, q_rank, q_norm_g, wq_p, cosv, sinv,
                  LOG2E * qk_dim ** -0.5, heads * LANES)
    kv = _rms_proj(h1, c_kv_off // kv_rank, kv_rank, kv_norm_g, wkv_p, cosv, sinv, 1.0, 0)
    y_d = _flash(q, 0, heads, kv, 0, heads, k2, heads)
    return _proj_ln((y_c, y_d), w_out.astype(BF16), x32, g, b, 1.0)


def kernel(x, positions, l0_ffn1_w1, l0_ffn1_w3, l0_ffn1_w2, l0_ln_ffn1_g, l0_ln_ffn1_b, l0_w_in, l0_b_f, l0_pool_w, l0_pool_scale, l0_w_out, l0_ln_mix_g, l0_ln_mix_b, l0_ffn2_w1, l0_ffn2_w3, l0_ffn2_w2, l0_ln_ffn2_g, l0_ln_ffn2_b, l1_ffn1_w1, l1_ffn1_w3, l1_ffn1_w2, l1_ln_ffn1_g, l1_ln_ffn1_b, l1_w_in, l1_conv_w, l1_conv_b, l1_conv_ln_g, l1_conv_ln_b, l1_q_norm_g, l1_w_uq, l1_kv_norm_g, l1_w_ukv, l1_w_out, l1_ln_mix_g, l1_ln_mix_b, l1_ffn2_w1, l1_ffn2_w3, l1_ffn2_w2, l1_ln_ffn2_g, l1_ln_ffn2_b):
    bsz, s, d = x.shape
    outs = []
    for bi in range(bsz):
        x32 = x.reshape(s, d) if bsz == 1 else x[bi]
        x16 = _to_bf16(x32)
        pos = positions.reshape(s) if bsz == 1 else positions[bi]
        x32, x16 = _ffn(x32, x16, l0_ffn1_w1, l0_ffn1_w3, l0_ffn1_w2, l0_ln_ffn1_g, l0_ln_ffn1_b)
        x32, x16 = _mixer_fox_pool(x32, x16, l0_w_in, l0_b_f, l0_pool_w, l0_pool_scale, l0_w_out,
                                   l0_ln_mix_g, l0_ln_mix_b)
        x32, x16 = _ffn(x32, x16, l0_ffn2_w1, l0_ffn2_w3, l0_ffn2_w2, l0_ln_ffn2_g, l0_ln_ffn2_b)
        x32, x16 = _ffn(x32, x16, l1_ffn1_w1, l1_ffn1_w3, l1_ffn1_w2, l1_ln_ffn1_g, l1_ln_ffn1_b)
        x32, x16 = _mixer_conv_mla(x32, x16, pos, l1_w_in, l1_conv_w, l1_conv_b, l1_conv_ln_g,
                                   l1_conv_ln_b, l1_q_norm_g, l1_w_uq, l1_kv_norm_g, l1_w_ukv,
                                   l1_w_out, l1_ln_mix_g, l1_ln_mix_b)
        x32, _ = _ffn(x32, x16, l1_ffn2_w1, l1_ffn2_w3, l1_ffn2_w2, l1_ln_ffn2_g, l1_ln_ffn2_b,
                      emit_bf16=False)
        outs.append(x32)
    return outs[0].reshape(1, s, d) if bsz == 1 else jnp.stack(outs, axis=0)
```

```python
import functools
import math

import numpy as np
import jax
import jax.numpy as jnp
from jax import lax
from jax.experimental import pallas as pl
from jax.experimental.pallas import tpu as pltpu

F32 = jnp.float32
BF16 = jnp.bfloat16

DEPTH = 2
ALPHA = (2 * DEPTH) ** 0.25
LN_EPS = 1e-5
RMS_EPS = 1e-6
HEAD_DIM = 128
MLA_ROPE_DIM = 64
ROPE_BASE = 10000.0
POOL_WINDOWS = (2, 4, 8, 16)
CONV_TAPS = 31

V7X_VMEM_LIMIT_BYTES = 56 * 1024 * 1024
LANES = 128
SUBLANES = 8
POOL_HALO = 16
CONV_HALO = 32
GATE_LANE_STRIDE = 32
FLASH_BLOCK = 1024
LN_ROWS = 128
LOG2E = math.log2(math.e)


def _pick(n, pref, align):
    best = None
    for d in range(align, min(n, pref) + 1, align):
        if n % d == 0:
            best = d
    return n if best is None else best


def _cparams(*sem):
    return pltpu.CompilerParams(dimension_semantics=sem, vmem_limit_bytes=V7X_VMEM_LIMIT_BYTES)


def _split3(x):
    hi = x.astype(BF16)
    r1 = x - hi.astype(F32)
    mid = r1.astype(BF16)
    lo = (r1 - mid.astype(F32)).astype(BF16)
    return hi, mid, lo


def _cast_body(x_ref, o_ref):
    o_ref[...] = x_ref[...].astype(o_ref.dtype)


def _to_bf16(x):
    s, d = x.shape
    tm = _pick(s, 512, SUBLANES)
    return pl.pallas_call(
        _cast_body,
        grid=(s // tm,),
        in_specs=[pl.BlockSpec((tm, d), lambda i: (i, 0))],
        out_specs=pl.BlockSpec((tm, d), lambda i: (i, 0)),
        out_shape=jax.ShapeDtypeStruct((s, d), BF16),
        compiler_params=_cparams("parallel"),
        name="cast_bf16",
    )(x)


def _gateup_body(x_ref, w1_ref, w3_ref, o_ref, *, f, ncols):
    tm, tn = o_ref.shape
    x = x_ref[...]
    for c0 in range(0, tn, ncols):
        a = jnp.dot(x, w1_ref[:, pl.ds(c0, ncols)], preferred_element_type=F32)
        b = jnp.dot(x, w3_ref[:, pl.ds(c0, ncols)], preferred_element_type=F32)
        hidden = a * jax.nn.sigmoid(a) * b
        col = pl.program_id(1) * tn + c0 + lax.broadcasted_iota(jnp.int32, (tm, ncols), 1)
        o_ref[:, pl.ds(c0, ncols)] = jnp.where(col < f, hidden, 0.0).astype(o_ref.dtype)


def _ffn_gateup(xb, w1, w3, fp):
    s, d = xb.shape
    f = w1.shape[1]
    tm = _pick(s, 1024, SUBLANES)
    tn = _pick(fp, 512, LANES)
    return pl.pallas_call(
        functools.partial(_gateup_body, f=f, ncols=_pick(tn, 256, LANES)),
        grid=(s // tm, fp // tn),
        in_specs=[pl.BlockSpec((tm, d), lambda i, j: (i, 0)),
                  pl.BlockSpec((d, tn), lambda i, j: (0, j)),
                  pl.BlockSpec((d, tn), lambda i, j: (0, j))],
        out_specs=pl.BlockSpec((tm, tn), lambda i, j: (i, j)),
        out_shape=jax.ShapeDtypeStruct((s, fp), BF16),
        compiler_params=_cparams("parallel", "arbitrary"),
        name="ffn_gateup",
    )(xb, w1, w3)


def _proj_ln_body(*refs, r, nk, nk_first, rows, ncols, emit_bf16):
    two_lhs = nk_first < nk
    ya_ref = refs[0]
    yb_ref = refs[1] if two_lhs else None
    w_ref, x_ref, g_ref, b_ref, o32_ref = refs[1 + two_lhs:6 + two_lhs]
    o16_ref = refs[6 + two_lhs] if emit_bf16 else None
    k = pl.program_id(1)

    @pl.when(k == 0)
    def _():
        o32_ref[...] = jnp.zeros(o32_ref.shape, F32)

    y = ya_ref[...]
    if two_lhs:
        y = jnp.where(k < nk_first, y, yb_ref[...])
    d = o32_ref.shape[1]
    for c0 in range(0, d, ncols):
        o32_ref[:, pl.ds(c0, ncols)] += jnp.dot(y, w_ref[:, pl.ds(c0, ncols)],
                                                preferred_element_type=F32)

    @pl.when(k == nk - 1)
    def _():
        g = g_ref[...]
        b = b_ref[...]
        tm = o32_ref.shape[0]

        def chunk(c, carry):
            r0 = pl.multiple_of(c * rows, rows)
            z = ALPHA * x_ref[pl.ds(r0, rows), :] + r * o32_ref[pl.ds(r0, rows), :]
            mu = jnp.mean(z, axis=-1, keepdims=True)
            zc = z - mu
            var = jnp.mean(zc * zc, axis=-1, keepdims=True)
            out = zc * lax.rsqrt(var + LN_EPS) * g + b
            o32_ref[pl.ds(r0, rows), :] = out
            if emit_bf16:
                o16_ref[pl.ds(r0, rows), :] = out.astype(BF16)
            return carry

        lax.fori_loop(0, tm // rows, chunk, 0)


def _proj_ln(ys, w, x, g, b, r, emit_bf16=True):
    s = ys[0].shape[0]
    kdim, d = w.shape
    tm = _pick(s, 512, SUBLANES)
    tk = _pick(math.gcd(*[y.shape[1] for y in ys]), 512, LANES)
    nk = kdim // tk
    nk_first = ys[0].shape[1] // tk
    y_specs = [pl.BlockSpec((tm, tk), lambda i, k: (i, jnp.minimum(k, nk_first - 1)))]
    if len(ys) == 2:
        y_specs.append(pl.BlockSpec((tm, tk), lambda i, k: (i, jnp.maximum(k - nk_first, 0))))
    rows = _pick(tm, LN_ROWS, SUBLANES)
    ncols = _pick(d, 1024, LANES)
    out_shape = [jax.ShapeDtypeStruct((s, d), F32)]
    out_specs = [pl.BlockSpec((tm, d), lambda i, k: (i, 0))]
    if emit_bf16:
        out_shape.append(jax.ShapeDtypeStruct((s, d), BF16))
        out_specs.append(pl.BlockSpec((tm, d), lambda i, k: (i, 0)))
    outs = pl.pallas_call(
        functools.partial(_proj_ln_body, r=r, nk=nk, nk_first=nk_first, rows=rows, ncols=ncols,
                          emit_bf16=emit_bf16),
        grid=(s // tm, nk),
        in_specs=y_specs + [pl.BlockSpec((tk, d), lambda i, k: (k, 0)),
                            pl.BlockSpec((tm, d), lambda i, k: (i, 0)),
                            pl.BlockSpec((1, d), lambda i, k: (0, 0)),
                            pl.BlockSpec((1, d), lambda i, k: (0, 0))],
        out_specs=out_specs,
        out_shape=out_shape,
        compiler_params=_cparams("parallel", "arbitrary"),
        name="proj_residual_ln",
    )(*ys, w, x, g.reshape(1, d), b.reshape(1, d))
    return outs if emit_bf16 else (outs[0], None)


def _proj_body(x_ref, w_ref, o_ref, *, scale, scaled_blocks):
    acc = jnp.dot(x_ref[...], w_ref[...], preferred_element_type=F32)
    if scaled_blocks:
        acc = acc * jnp.where(pl.program_id(1) < scaled_blocks, scale, 1.0).astype(F32)
    o_ref[...] = acc.astype(o_ref.dtype)


def _proj(xb, w, out_dtype, tn_pref, scale=1.0, scaled_cols=0, n=None):
    s, d = xb.shape
    n = w.shape[1] if n is None else n
    tm = _pick(s, 1024, SUBLANES)
    tn = _pick(n, tn_pref, LANES)
    if scaled_cols:
        tn = math.gcd(tn, scaled_cols)
    return pl.pallas_call(
        functools.partial(_proj_body, scale=scale, scaled_blocks=scaled_cols // tn),
        grid=(s // tm, n // tn),
        in_specs=[pl.BlockSpec((tm, d), lambda i, j: (i, 0)),
                  pl.BlockSpec((d, tn), lambda i, j: (0, j))],
        out_specs=pl.BlockSpec((tm, tn), lambda i, j: (i, j)),
        out_shape=jax.ShapeDtypeStruct((s, n), out_dtype),
        compiler_params=_cparams("parallel", "arbitrary"),
        name="proj",
    )(xb, w)


def _gates_body(f_ref, b_ref, o_ref, carry_ref, *, heads):
    i = pl.program_id(0)
    tm = f_ref.shape[0]

    @pl.when(i == 0)
    def _():
        carry_ref[...] = jnp.zeros_like(carry_ref)

    lane = lax.broadcasted_iota(jnp.int32, (tm, LANES), 1)
    xv = f_ref[...] + b_ref[...]
    logf = jnp.minimum(xv, 0.0) - jnp.log1p(jnp.exp(-jnp.abs(xv)))
    logf = jnp.where(lane < heads, logf, 0.0)
    row = lax.broadcasted_iota(jnp.int32, (tm, tm), 0)
    col = lax.broadcasted_iota(jnp.int32, (tm, tm), 1)
    tril = jnp.where(row >= col, 1.0, 0.0).astype(BF16)
    hi, mid, lo = _split3(logf)
    c = (jnp.dot(tril, hi, preferred_element_type=F32)
         + jnp.dot(tril, mid, preferred_element_type=F32)
         + jnp.dot(tril, lo, preferred_element_type=F32)) + carry_ref[...]
    carry_ref[...] = c[tm - 1:tm, :]
    nhi, nmid, nlo = _split3(-LOG2E * c)
    packed = jnp.where(lane < GATE_LANE_STRIDE, nhi.astype(F32),
                       jnp.where(lane < 2 * GATE_LANE_STRIDE,
                                 pltpu.roll(nmid.astype(F32), GATE_LANE_STRIDE, 1),
                                 pltpu.roll(nlo.astype(F32), 2 * GATE_LANE_STRIDE, 1)))
    o_ref[...] = packed.astype(BF16)


def _fox_gates(uf, ucols, b_f_pad, heads):
    s = uf.shape[0]
    tm = _pick(s, 512, SUBLANES)
    fblk = ucols // LANES
    return pl.pallas_call(
        functools.partial(_gates_body, heads=heads),
        grid=(s // tm,),
        in_specs=[pl.BlockSpec((tm, LANES), lambda i: (i, fblk)),
                  pl.BlockSpec((1, LANES), lambda i: (0, 0))],
        out_specs=pl.BlockSpec((tm, LANES), lambda i: (i, 0)),
        out_shape=jax.ShapeDtypeStruct((s, LANES), BF16),
        scratch_shapes=[pltpu.VMEM((1, LANES), F32)],
        compiler_params=_cparams("arbitrary"),
        name="fox_gates",
    )(uf, b_f_pad)


def _flash_body(*refs, tb, fox, rchunk):
    n_in = 4 if fox else 5
    q1_ref = refs[0]
    q2_ref = None if fox else refs[1]
    k1_ref, k2_ref, v_ref = refs[n_in - 3:n_in]
    o_ref = refs[n_in]
    q_sc, s0, s1, p0, p1, a0, a1, m_sc, acc_sc = refs[n_in + 1:]
    s_bufs, p_bufs, a_bufs = (s0, s1), (p0, p1), (a0, a1)
    h = pl.program_id(0)
    qi = pl.program_id(1)

    q_sc[:, pl.ds(0, HEAD_DIM)] = q1_ref[...]
    if fox:
        lane = lax.broadcasted_iota(jnp.int32, (tb, LANES), 1)
        sel = (lane < 3 * GATE_LANE_STRIDE) & (jnp.bitwise_and(lane, GATE_LANE_STRIDE - 1) == h)
        q_sc[:, pl.ds(HEAD_DIM, LANES)] = jnp.where(sel, 1.0, 0.0).astype(BF16)
    else:
        q_sc[:, pl.ds(HEAD_DIM, LANES)] = q2_ref[...]
    m_sc[...] = jnp.full(m_sc.shape, -jnp.inf, F32)
    acc_sc[...] = jnp.zeros(acc_sc.shape, F32)
    p1[...] = jnp.zeros(p1.shape, BF16)
    a1[...] = jnp.ones(a1.shape, F32)
    ones = jnp.ones((tb, LANES), BF16)
    hb = tb // 2

    def qk(b, s_dst):
        k0 = pl.multiple_of(b * tb, tb)
        kk = jnp.concatenate([k1_ref[pl.ds(k0, tb), :], k2_ref[pl.ds(k0, tb), :]], axis=1)
        s_dst[...] = lax.dot_general(q_sc[...], kk, (((1,), (1,)), ((), ())),
                                     preferred_element_type=F32)

    def pv(b, p_src, a_src):
        k0 = pl.multiple_of(jnp.maximum(b, 0) * tb, tb)
        vv = jnp.concatenate([v_ref[pl.ds(k0, tb), :], ones], axis=1)
        acc_sc[...] = a_src[...] * acc_sc[...] + jnp.dot(p_src[...], vv, preferred_element_type=F32)

    def sm(s_src, p_dst, a_dst, diagonal):
        for r0 in range(0, tb, rchunk):
            ncol = hb if diagonal and r0 + rchunk <= hb else tb
            s = s_src[pl.ds(r0, rchunk), pl.ds(0, ncol)]
            if diagonal:
                row = r0 + lax.broadcasted_iota(jnp.int32, (rchunk, ncol), 0)
                col = lax.broadcasted_iota(jnp.int32, (rchunk, ncol), 1)
                s = jnp.where(row >= col, s, -jnp.inf)
            m_prev = m_sc[pl.ds(r0, rchunk), :]
            m_new = jnp.maximum(m_prev, jnp.max(s, axis=1, keepdims=True))
            a_dst[pl.ds(r0, rchunk), :] = jnp.exp2(m_prev - m_new)
            p_dst[pl.ds(r0, rchunk), pl.ds(0, ncol)] = jnp.exp2(s - m_new).astype(BF16)
            m_sc[pl.ds(r0, rchunk), :] = m_new

    def pv_diagonal(b, p_src, a_src):
        k0 = pl.multiple_of(b * tb, tb)
        vv = jnp.concatenate([v_ref[pl.ds(k0, tb), :], ones], axis=1)
        acc_sc[pl.ds(0, hb), :] = (a_src[pl.ds(0, hb), :] * acc_sc[pl.ds(0, hb), :]
                                   + jnp.dot(p_src[pl.ds(0, hb), pl.ds(0, hb)], vv[:hb],
                                             preferred_element_type=F32))
        acc_sc[pl.ds(hb, hb), :] = (a_src[pl.ds(hb, hb), :] * acc_sc[pl.ds(hb, hb), :]
                                    + jnp.dot(p_src[pl.ds(hb, hb), :], vv,
                                              preferred_element_type=F32))

    def half(b, cur):
        nxt = 1 - cur
        pv(b - 1, p_bufs[nxt], a_bufs[nxt])
        qk(b + 1, s_bufs[nxt])
        sm(s_bufs[cur], p_bufs[cur], a_bufs[cur], False)

    def last(cur):
        pv(qi - 1, p_bufs[1 - cur], a_bufs[1 - cur])
        sm(s_bufs[cur], p_bufs[cur], a_bufs[cur], True)
        pv_diagonal(qi, p_bufs[cur], a_bufs[cur])
        o_ref[...] = (acc_sc[:, pl.ds(0, HEAD_DIM)]
                      / acc_sc[:, pl.ds(HEAD_DIM, LANES)]).astype(o_ref.dtype)

    qk(0, s0)

    def pair(i, carry):
        half(2 * i, 0)
        half(2 * i + 1, 1)
        return carry

    lax.fori_loop(0, lax.shift_right_logical(qi, 1), pair, 0)
    odd = jnp.bitwise_and(qi, 1)

    @pl.when(odd == 1)
    def _():
        half(qi - 1, 0)
        last(1)

    @pl.when(odd == 0)
    def _():
        last(0)


def _flash(q, q1_blk, q2_blk, kv, k1_blk, v_blk, k2, heads, tb_pref=FLASH_BLOCK):
    s = q.shape[0]
    tb = _pick(s, tb_pref, LANES)
    rchunk = _pick(tb, 32, SUBLANES)
    assert tb % (2 * LANES) == 0 and (tb // 2) % rchunk == 0
    fox = q2_blk is None
    in_specs = [pl.BlockSpec((tb, HEAD_DIM), lambda h, i: (i, q1_blk + h))]
    args = [q]
    if not fox:
        in_specs.append(pl.BlockSpec((tb, HEAD_DIM), lambda h, i: (i, q2_blk + h)))
        args.append(q)
    in_specs += [pl.BlockSpec((s, HEAD_DIM), lambda h, i: (0, k1_blk + h)),
                 pl.BlockSpec((s, LANES), lambda h, i: (0, 0)),
                 pl.BlockSpec((s, HEAD_DIM), lambda h, i: (0, v_blk + h))]
    args += [kv, k2, kv]
    score = pltpu.VMEM((tb, tb), F32)
    prob = pltpu.VMEM((tb, tb), BF16)
    col = pltpu.VMEM((tb, 1), F32)
    return pl.pallas_call(
        functools.partial(_flash_body, tb=tb, fox=fox, rchunk=rchunk),
        grid=(heads, s // tb),
        in_specs=in_specs,
        out_specs=pl.BlockSpec((tb, HEAD_DIM), lambda h, i: (i, h)),
        out_shape=jax.ShapeDtypeStruct((s, heads * HEAD_DIM), BF16),
        scratch_shapes=[pltpu.VMEM((tb, HEAD_DIM + LANES), BF16), score, score, prob, prob,
                        col, col, col, pltpu.VMEM((tb, HEAD_DIM + LANES), F32)],
        compiler_params=_cparams("parallel", "arbitrary"),
        name="flash_fox" if fox else "flash_mla",
    )(*args)


def _pool_body(u_ref, halo_ref, w_ref, sc_ref, o_ref, buf, *, groups, gdim):
    i = pl.program_id(0)
    tm = u_ref.shape[0]
    buf[pl.ds(0, POOL_HALO), :] = jnp.where(i == 0, 0.0, halo_ref[...])
    buf[pl.ds(POOL_HALO, tm), :] = u_ref[...]
    t = (i * tm + lax.broadcasted_iota(jnp.int32, (tm, 1), 0)).astype(F32)
    for g in range(groups):
        win = POOL_WINDOWS[g]
        c0 = g * gdim
        tot = buf[pl.ds(POOL_HALO, tm), pl.ds(c0, gdim)]
        cur = tot
        for lag in range(1, win):
            tot = tot + buf[pl.ds(POOL_HALO - lag, tm), pl.ds(c0, gdim)]
        count = jnp.minimum(t + 1.0, float(win))
        pooled = tot / count - cur
        mixed = jnp.dot(pooled.astype(BF16), w_ref[g], preferred_element_type=F32)
        o_ref[:, pl.ds(c0, gdim)] = (mixed * sc_ref[:, pl.ds(c0, gdim)]).astype(o_ref.dtype)


def _pool(uf, pool_w_b, pool_scale):
    s = uf.shape[0]
    groups, gdim, _ = pool_w_b.shape
    c = groups * gdim
    tm = _pick(s, 256, POOL_HALO)
    hb = tm // POOL_HALO
    return pl.pallas_call(
        functools.partial(_pool_body, groups=groups, gdim=gdim),
        grid=(s // tm,),
        in_specs=[pl.BlockSpec((tm, c), lambda i: (i, 0)),
                  pl.BlockSpec((POOL_HALO, c), lambda i: (jnp.maximum(i * hb - 1, 0), 0)),
                  pl.BlockSpec((groups, gdim, gdim), lambda i: (0, 0, 0)),
                  pl.BlockSpec((1, c), lambda i: (0, 0))],
        out_specs=pl.BlockSpec((tm, c), lambda i: (i, 0)),
        out_shape=jax.ShapeDtypeStruct((s, c), BF16),
        scratch_shapes=[pltpu.VMEM((tm + POOL_HALO, c), F32)],
        compiler_params=_cparams("parallel"),
        name="multiscale_pool",
    )(uf, uf, pool_w_b, pool_scale.reshape(1, c))


def _conv_body(a_ref, g_ref, ha_ref, hg_ref, w_ref, cb_ref, lg_ref, lb_ref, o_ref, buf, sh, ybuf, *, rows, lanes):
    i = pl.program_id(0)
    tm, c = a_ref.shape
    halo = ha_ref[...] * jax.nn.sigmoid(hg_ref[...])
    buf[pl.ds(0, CONV_HALO), :] = jnp.where(i == 0, 0.0, halo)
    buf[pl.ds(CONV_HALO, tm), :] = a_ref[...] * jax.nn.sigmoid(g_ref[...])
    span = sh.shape[1]
    for b in range(1, SUBLANES):
        sh[b - 1] = buf[pl.ds(b, span), :]
    base = CONV_HALO - (CONV_TAPS - 1)
    for r0 in range(0, tm, rows):
        for c0 in range(0, c, lanes):
            acc = jnp.zeros((rows, lanes), F32) + cb_ref[:, pl.ds(c0, lanes)]
            for j in range(CONV_TAPS):
                hi, lo = divmod(base + j, SUBLANES)
                if lo == 0:
                    win = buf[pl.ds(r0 + SUBLANES * hi, rows), pl.ds(c0, lanes)]
                else:
                    win = sh[lo - 1, pl.ds(r0 + SUBLANES * hi, rows), pl.ds(c0, lanes)]
                acc = acc + w_ref[pl.ds(j, 1), pl.ds(c0, lanes)] * win
            ybuf[pl.ds(r0, rows), pl.ds(c0, lanes)] = acc
    y = ybuf[...]
    mu = jnp.mean(y, axis=-1, keepdims=True)
    yc = y - mu
    var = jnp.mean(yc * yc, axis=-1, keepdims=True)
    z = yc * lax.rsqrt(var + LN_EPS) * lg_ref[...] + lb_ref[...]
    o_ref[...] = (z * jax.nn.sigmoid(z)).astype(o_ref.dtype)


def _conformer_conv(h1, cw, conv_w_p, conv_b, ln_g, ln_b):
    s = h1.shape[0]
    tm = _pick(s, 128, CONV_HALO)
    hb = tm // CONV_HALO
    rows = _pick(tm, 32, SUBLANES)
    lanes = _pick(cw, 256, LANES)
    nblk = 1
    vec = lambda a: a.reshape(1, cw)
    return pl.pallas_call(
        functools.partial(_conv_body, rows=rows, lanes=lanes),
        grid=(s // tm,),
        in_specs=[pl.BlockSpec((tm, cw), lambda i: (i, 0)),
                  pl.BlockSpec((tm, cw), lambda i: (i, nblk)),
                  pl.BlockSpec((CONV_HALO, cw), lambda i: (jnp.maximum(i * hb - 1, 0), 0)),
                  pl.BlockSpec((CONV_HALO, cw), lambda i: (jnp.maximum(i * hb - 1, 0), nblk)),
                  pl.BlockSpec((CONV_HALO, cw), lambda i: (0, 0)),
                  pl.BlockSpec((1, cw), lambda i: (0, 0)),
                  pl.BlockSpec((1, cw), lambda i: (0, 0)),
                  pl.BlockSpec((1, cw), lambda i: (0, 0))],
        out_specs=pl.BlockSpec((tm, cw), lambda i: (i, 0)),
        out_shape=jax.ShapeDtypeStruct((s, cw), BF16),
        scratch_shapes=[pltpu.VMEM((tm + CONV_HALO, cw), F32),
                        pltpu.VMEM((SUBLANES - 1, tm + CONV_HALO - SUBLANES, cw), F32),
                        pltpu.VMEM((tm, cw), F32)],
        compiler_params=_cparams("parallel"),
        name="conformer_conv",
    )(h1, h1, h1, h1, conv_w_p, vec(conv_b), vec(ln_g), vec(ln_b))


def _rope128(r, cosv, sinv, lane):
    half = MLA_ROPE_DIM // 2
    s1 = jnp.where(lane < half, -sinv, 0.0)
    s2 = jnp.where((lane >= half) & (lane < 2 * half), sinv, 0.0)
    return r * cosv + pltpu.roll(r, LANES - half, 1) * s1 + pltpu.roll(r, half, 1) * s2


def _rope_k_body(pos_ref, invf_ref, kpe_ref, cos_ref, sin_ref, k2_ref):
    tm = pos_ref.shape[0]
    ang = pos_ref[...].astype(F32) * invf_ref[...]
    cosv = jnp.cos(ang)
    sinv = jnp.sin(ang)
    cos_ref[...] = cosv
    sin_ref[...] = sinv
    lane = lax.broadcasted_iota(jnp.int32, (tm, LANES), 1)
    k2_ref[...] = _rope128(kpe_ref[...], cosv, sinv, lane).astype(BF16)


def _rope_k(positions_col, invf, h1, kpe_block):
    s = h1.shape[0]
    tm = _pick(s, 512, SUBLANES)
    tab = jax.ShapeDtypeStruct((s, LANES), F32)
    return pl.pallas_call(
        _rope_k_body,
        grid=(s // tm,),
        in_specs=[pl.BlockSpec((tm, 1), lambda i: (i, 0)),
                  pl.BlockSpec((1, LANES), lambda i: (0, 0)),
                  pl.BlockSpec((tm, LANES), lambda i: (i, kpe_block))],
        out_specs=[pl.BlockSpec((tm, LANES), lambda i: (i, 0))] * 3,
        out_shape=[tab, tab, jax.ShapeDtypeStruct((s, LANES), BF16)],
        compiler_params=_cparams("parallel"),
        name="rope_tables_key",
    )(positions_col, invf, h1)


def _rms_proj_body(c_ref, g_ref, w_ref, cos_ref, sin_ref, o_ref, xn_ref, *, scale, rope_from):
    j = pl.program_id(1)
    tm, tn = o_ref.shape

    @pl.when(j == 0)
    def _():
        xf = c_ref[...]
        ms = jnp.mean(xf * xf, axis=-1, keepdims=True)
        xn_ref[...] = (xf * lax.rsqrt(ms + RMS_EPS) * g_ref[...]).astype(BF16)

    acc = jnp.dot(xn_ref[...], w_ref[...], preferred_element_type=F32)

    def plain():
        o_ref[...] = (acc * scale).astype(o_ref.dtype)

    if rope_from is None:
        plain()
        return

    pl.when(j < rope_from)(plain)

    @pl.when(j >= rope_from)
    def _():
        lane = lax.broadcasted_iota(jnp.int32, (tm, LANES), 1)
        cosv = cos_ref[...]
        sinv = sin_ref[...]
        for c0 in range(0, tn, LANES):
            r = _rope128(acc[:, c0:c0 + LANES], cosv, sinv, lane)
            o_ref[:, pl.ds(c0, LANES)] = (r * scale).astype(o_ref.dtype)


def _rms_proj(h1, cblock, rank, g, w, cosv, sinv, scale, rope_cols):
    s = h1.shape[0]
    n = w.shape[1]
    tm = _pick(s, 1024, SUBLANES)
    tn = _pick(n, 1024, LANES)
    if rope_cols:
        tn = math.gcd(tn, n - rope_cols)
    rope_from = (n - rope_cols) // tn if rope_cols else None
    return pl.pallas_call(
        functools.partial(_rms_proj_body, scale=scale, rope_from=rope_from),
        grid=(s // tm, n // tn),
        in_specs=[pl.BlockSpec((tm, rank), lambda i, j: (i, cblock)),
                  pl.BlockSpec((1, rank), lambda i, j: (0, 0)),
                  pl.BlockSpec((rank, tn), lambda i, j: (0, j)),
                  pl.BlockSpec((tm, LANES), lambda i, j: (i, 0)),
                  pl.BlockSpec((tm, LANES), lambda i, j: (i, 0))],
        out_specs=pl.BlockSpec((tm, tn), lambda i, j: (i, j)),
        out_shape=jax.ShapeDtypeStruct((s, n), BF16),
        scratch_shapes=[pltpu.VMEM((tm, rank), BF16)],
        compiler_params=_cparams("parallel", "arbitrary"),
        name="rms_up_proj",
    )(h1, g.reshape(1, rank), w, cosv, sinv)


def _ffn(x32, x16, w1, w3, w2, g, b, emit_bf16=True):
    d, f = w1.shape
    fp = -(-f // 512) * 512
    w2p = jnp.pad(w2.astype(BF16), ((0, fp - f), (0, 0)))
    hidden = _ffn_gateup(x16, w1.astype(BF16), w3.astype(BF16), fp)
    return _proj_ln((hidden,), w2p, x32, g, b, 0.5, emit_bf16)


def _mixer_fox_pool(x32, x16, w_in, b_f, pool_w, pool_scale, w_out, g, b):
    d = x32.shape[1]
    heads = b_f.shape[0]
    fw = heads * HEAD_DIM
    groups, gdim, _ = pool_w.shape
    pw = groups * gdim
    w_uf = jnp.concatenate([w_in[:, 3 * fw + heads:], w_in[:, 3 * fw:3 * fw + heads],
                            jnp.zeros((d, LANES - heads), F32)], axis=1).astype(BF16)
    b_f_pad = jnp.pad(b_f, (0, LANES - heads)).reshape(1, LANES)

    qkv = _proj(x16, w_in.astype(BF16), BF16, 1024, scale=LOG2E * HEAD_DIM ** -0.5, scaled_cols=fw,
                n=3 * fw)
    uf = _proj(x16, w_uf, F32, pw + LANES)
    gate_parts = _fox_gates(uf, pw, b_f_pad, heads)
    y_a = _flash(qkv, 0, None, qkv, heads, 2 * heads, gate_parts, heads)
    y_b = _pool(uf, pool_w.astype(BF16), pool_scale)
    return _proj_ln((y_a, y_b), w_out.astype(BF16), x32, g, b, 1.0)


def _mixer_conv_mla(x32, x16, positions, w_in, conv_w, conv_b, conv_ln_g, conv_ln_b,
                    q_norm_g, w_uq, kv_norm_g, w_ukv, w_out, g, b):
    s, d = x32.shape
    cw = conv_b.shape[0]
    q_rank = q_norm_g.shape[0]
    kv_rank = kv_norm_g.shape[0]
    qk_dim = HEAD_DIM + MLA_ROPE_DIM
    heads = w_uq.shape[1] // qk_dim
    hw = heads * HEAD_DIM
    n_in = w_in.shape[1]
    n_in_p = -(-(n_in + MLA_ROPE_DIM) // 768) * 768
    w_in_p = jnp.pad(w_in.astype(BF16), ((0, 0), (0, n_in_p - n_in)))
    c_q_off = 2 * cw
    c_kv_off = c_q_off + q_rank
    kpe_off = c_kv_off + kv_rank
    assert c_q_off % q_rank == 0 and c_kv_off % kv_rank == 0 and kpe_off % LANES == 0

    wq = w_uq.reshape(q_rank, heads, qk_dim)
    wq_rope = jnp.pad(wq[:, :, HEAD_DIM:], ((0, 0), (0, 0), (0, LANES - MLA_ROPE_DIM)))
    wq_p = jnp.concatenate([wq[:, :, :HEAD_DIM].reshape(q_rank, hw),
                            wq_rope.reshape(q_rank, heads * LANES)], axis=1).astype(BF16)
    wkv = w_ukv.reshape(kv_rank, heads, 2 * HEAD_DIM)
    wkv_p = jnp.concatenate([wkv[:, :, :HEAD_DIM].reshape(kv_rank, hw),
                             wkv[:, :, HEAD_DIM:].reshape(kv_rank, hw)], axis=1).astype(BF16)
    conv_w_p = jnp.pad(conv_w.reshape(CONV_TAPS, cw), ((0, CONV_HALO - CONV_TAPS), (0, 0)))
    half = MLA_ROPE_DIM // 2
    invf = np.zeros((1, LANES), np.float32)
    freqs = ROPE_BASE ** (-np.arange(half, dtype=np.float32) / half)
    invf[0, :half] = freqs
    invf[0, half:2 * half] = freqs
    invf = jnp.asarray(invf)

    h1 = _proj(x16, w_in_p, F32, 768)
    y_c = _conformer_conv(h1, cw, conv_w_p, conv_b, conv_ln_g, conv_ln_b)
    cosv, sinv, k2 = _rope_k(positions.reshape(s, 1), invf, h1, kpe_off // LANES)
    q = _rms_proj(h1, c_q_off // q_rank, q_rank, q_norm_g, wq_p, cosv, sinv,
                  LOG2E * qk_dim ** -0.5, heads * LANES)
    kv = _rms_proj(h1, c_kv_off // kv_rank, kv_rank, kv_norm_g, wkv_p, cosv, sinv, 1.0, 0)
    y_d = _flash(q, 0, heads, kv, 0, heads, k2, heads)
    return _proj_ln((y_c, y_d), w_out.astype(BF16), x32, g, b, 1.0)


def kernel(x, positions, l0_ffn1_w1, l0_ffn1_w3, l0_ffn1_w2, l0_ln_ffn1_g, l0_ln_ffn1_b, l0_w_in, l0_b_f, l0_pool_w, l0_pool_scale, l0_w_out, l0_ln_mix_g, l0_ln_mix_b, l0_ffn2_w1, l0_ffn2_w3, l0_ffn2_w2, l0_ln_ffn2_g, l0_ln_ffn2_b, l1_ffn1_w1, l1_ffn1_w3, l1_ffn1_w2, l1_ln_ffn1_g, l1_ln_ffn1_b, l1_w_in, l1_conv_w, l1_conv_b, l1_conv_ln_g, l1_conv_ln_b, l1_q_norm_g, l1_w_uq, l1_kv_norm_g, l1_w_ukv, l1_w_out, l1_ln_mix_g, l1_ln_mix_b, l1_ffn2_w1, l1_ffn2_w3, l1_ffn2_w2, l1_ln_ffn2_g, l1_ln_ffn2_b):
    bsz, s, d = x.shape
    outs = []
    for bi in range(bsz):
        x32 = x.reshape(s, d) if bsz == 1 else x[bi]
        x16 = _to_bf16(x32)
        pos = positions.reshape(s) if bsz == 1 else positions[bi]
        x32, x16 = _ffn(x32, x16, l0_ffn1_w1, l0_ffn1_w3, l0_ffn1_w2, l0_ln_ffn1_g, l0_ln_ffn1_b)
        x32, x16 = _mixer_fox_pool(x32, x16, l0_w_in, l0_b_f, l0_pool_w, l0_pool_scale, l0_w_out,
                                   l0_ln_mix_g, l0_ln_mix_b)
        x32, x16 = _ffn(x32, x16, l0_ffn2_w1, l0_ffn2_w3, l0_ffn2_w2, l0_ln_ffn2_g, l0_ln_ffn2_b)
        x32, x16 = _ffn(x32, x16, l1_ffn1_w1, l1_ffn1_w3, l1_ffn1_w2, l1_ln_ffn1_g, l1_ln_ffn1_b)
        x32, x16 = _mixer_conv_mla(x32, x16, pos, l1_w_in, l1_conv_w, l1_conv_b, l1_conv_ln_g,
                                   l1_conv_ln_b, l1_q_norm_g, l1_w_uq, l1_kv_norm_g, l1_w_ukv,
                                   l1_w_out, l1_ln_mix_g, l1_ln_mix_b)
        x32, _ = _ffn(x32, x16, l1_ffn2_w1, l1_ffn2_w3, l1_ffn2_w2, l1_ln_ffn2_g, l1_ln_ffn2_b,
                      emit_bf16=False)
        outs.append(x32)
    return outs[0].reshape(1, s, d) if bsz == 1 else jnp.stack(outs, axis=0)
```

```python
import functools
import math

import numpy as np
import jax
import jax.numpy as jnp
from jax import lax
from jax.experimental import pallas as pl
from jax.experimental.pallas import tpu as pltpu

F32 = jnp.float32
BF16 = jnp.bfloat16

DEPTH = 2
ALPHA = (2 * DEPTH) ** 0.25
LN_EPS = 1e-5
RMS_EPS = 1e-6
HEAD_DIM = 128
MLA_ROPE_DIM = 64
ROPE_BASE = 10000.0
POOL_WINDOWS = (2, 4, 8, 16)
CONV_TAPS = 31

V7X_VMEM_LIMIT_BYTES = 56 * 1024 * 1024
V7X_VMEM_LIMIT_F32_WEIGHTS_BYTES = 60 * 1024 * 1024
LANES = 128
SUBLANES = 8
POOL_HALO = 16
CONV_HALO = 32
GATE_LANE_STRIDE = 32
FLASH_BLOCK = 1024
LN_ROWS = 128
LOG2E = math.log2(math.e)


def _pick(n, pref, align):
    best = None
    for d in range(align, min(n, pref) + 1, align):
        if n % d == 0:
            best = d
    return n if best is None else best


def _cparams(*sem, vmem_limit_bytes=V7X_VMEM_LIMIT_BYTES):
    return pltpu.CompilerParams(dimension_semantics=sem, vmem_limit_bytes=vmem_limit_bytes)


def _split3(x):
    hi = x.astype(BF16)
    r1 = x - hi.astype(F32)
    mid = r1.astype(BF16)
    lo = (r1 - mid.astype(F32)).astype(BF16)
    return hi, mid, lo


def _cast_body(x_ref, o_ref, *, rows_valid):
    tm = x_ref.shape[0]
    row = pl.program_id(0) * tm + lax.broadcasted_iota(jnp.int32, x_ref.shape, 0)
    o_ref[...] = jnp.where(row < rows_valid, x_ref[...], 0.0).astype(o_ref.dtype)


def _to_bf16(x, rows_out=None):
    s, d = x.shape
    rows_out = s if rows_out is None else rows_out
    tm = _pick(rows_out, 512, SUBLANES)
    return pl.pallas_call(
        functools.partial(_cast_body, rows_valid=s),
        grid=(rows_out // tm,),
        in_specs=[pl.BlockSpec((tm, d), lambda i: (i, 0))],
        out_specs=pl.BlockSpec((tm, d), lambda i: (i, 0)),
        out_shape=jax.ShapeDtypeStruct((rows_out, d), BF16),
        compiler_params=_cparams("parallel"),
        name="cast_bf16",
    )(x)


def _gateup_body(x_ref, w1_ref, w3_ref, o_ref, *, f, ncols):
    tm, tn = o_ref.shape
    x = x_ref[...]
    for c0 in range(0, tn, ncols):
        a = jnp.dot(x, w1_ref[:, pl.ds(c0, ncols)].astype(BF16), preferred_element_type=F32)
        b = jnp.dot(x, w3_ref[:, pl.ds(c0, ncols)].astype(BF16), preferred_element_type=F32)
        hidden = a * jax.nn.sigmoid(a) * b
        col = pl.program_id(1) * tn + c0 + lax.broadcasted_iota(jnp.int32, (tm, ncols), 1)
        o_ref[:, pl.ds(c0, ncols)] = jnp.where(col < f, hidden, 0.0).astype(o_ref.dtype)


def _ffn_gateup(xb, w1, w3, fp):
    s, d = xb.shape
    f = w1.shape[1]
    tm = _pick(s, 1024, SUBLANES)
    tn = _pick(fp, 512, LANES)
    return pl.pallas_call(
        functools.partial(_gateup_body, f=f, ncols=_pick(tn, 256, LANES)),
        grid=(s // tm, fp // tn),
        in_specs=[pl.BlockSpec((tm, d), lambda i, j: (i, 0)),
                  pl.BlockSpec((d, tn), lambda i, j: (0, j)),
                  pl.BlockSpec((d, tn), lambda i, j: (0, j))],
        out_specs=pl.BlockSpec((tm, tn), lambda i, j: (i, j)),
        out_shape=jax.ShapeDtypeStruct((s, fp), BF16),
        compiler_params=_cparams("parallel", "arbitrary",
                                 vmem_limit_bytes=V7X_VMEM_LIMIT_F32_WEIGHTS_BYTES),
        name="ffn_gateup",
    )(xb, w1, w3)


def _proj_ln_body(*refs, r, nk, nk_first, rows, ncols, emit_bf16):
    two_lhs = nk_first < nk
    ya_ref = refs[0]
    yb_ref = refs[1] if two_lhs else None
    w_ref, x_ref, g_ref, b_ref, o32_ref = refs[1 + two_lhs:6 + two_lhs]
    o16_ref = refs[6 + two_lhs] if emit_bf16 else None
    k = pl.program_id(1)

    @pl.when(k == 0)
    def _():
        o32_ref[...] = jnp.zeros(o32_ref.shape, F32)

    y = ya_ref[...]
    if two_lhs:
        y = jnp.where(k < nk_first, y, yb_ref[...])
    d = o32_ref.shape[1]
    for c0 in range(0, d, ncols):
        o32_ref[:, pl.ds(c0, ncols)] += jnp.dot(y, w_ref[:, pl.ds(c0, ncols)],
                                                preferred_element_type=F32)

    @pl.when(k == nk - 1)
    def _():
        g = g_ref[...]
        b = b_ref[...]
        tm = o32_ref.shape[0]

        def chunk(c, carry):
            r0 = pl.multiple_of(c * rows, rows)
            z = ALPHA * x_ref[pl.ds(r0, rows), :] + r * o32_ref[pl.ds(r0, rows), :]
            mu = jnp.mean(z, axis=-1, keepdims=True)
            zc = z - mu
            var = jnp.mean(zc * zc, axis=-1, keepdims=True)
            out = zc * lax.rsqrt(var + LN_EPS) * g + b
            o32_ref[pl.ds(r0, rows), :] = out
            if emit_bf16:
                o16_ref[pl.ds(r0, rows), :] = out.astype(BF16)
            return carry

        lax.fori_loop(0, tm // rows, chunk, 0)


def _proj_ln(ys, w, x, g, b, r, emit_bf16=True):
    s = ys[0].shape[0]
    kdim, d = w.shape
    tm = _pick(s, 512, SUBLANES)
    tk = _pick(math.gcd(*[y.shape[1] for y in ys]), 512, LANES)
    nk = kdim // tk
    nk_first = ys[0].shape[1] // tk
    y_specs = [pl.BlockSpec((tm, tk), lambda i, k: (i, jnp.minimum(k, nk_first - 1)))]
    if len(ys) == 2:
        y_specs.append(pl.BlockSpec((tm, tk), lambda i, k: (i, jnp.maximum(k - nk_first, 0))))
    rows = _pick(tm, LN_ROWS, SUBLANES)
    ncols = _pick(d, 1024, LANES)
    out_shape = [jax.ShapeDtypeStruct((s, d), F32)]
    out_specs = [pl.BlockSpec((tm, d), lambda i, k: (i, 0))]
    if emit_bf16:
        out_shape.append(jax.ShapeDtypeStruct((s, d), BF16))
        out_specs.append(pl.BlockSpec((tm, d), lambda i, k: (i, 0)))
    outs = pl.pallas_call(
        functools.partial(_proj_ln_body, r=r, nk=nk, nk_first=nk_first, rows=rows, ncols=ncols,
                          emit_bf16=emit_bf16),
        grid=(s // tm, nk),
        in_specs=y_specs + [pl.BlockSpec((tk, d), lambda i, k: (k, 0)),
                            pl.BlockSpec((tm, d), lambda i, k: (i, 0)),
                            pl.BlockSpec((1, d), lambda i, k: (0, 0)),
                            pl.BlockSpec((1, d), lambda i, k: (0, 0))],
        out_specs=out_specs,
        out_shape=out_shape,
        compiler_params=_cparams("parallel", "arbitrary"),
        name="proj_residual_ln",
    )(*ys, w, x, g.reshape(1, d), b.reshape(1, d))
    return outs if emit_bf16 else (outs[0], None)


def _proj_body(x_ref, w_ref, o_ref, *, scale, scaled_blocks, ncols):
    x = x_ref[...]
    tn = o_ref.shape[1]
    for c0 in range(0, tn, ncols):
        acc = jnp.dot(x, w_ref[:, pl.ds(c0, ncols)].astype(BF16), preferred_element_type=F32)
        if scaled_blocks:
            acc = acc * jnp.where(pl.program_id(1) < scaled_blocks, scale, 1.0).astype(F32)
        o_ref[:, pl.ds(c0, ncols)] = acc.astype(o_ref.dtype)


def _proj(xb, w, out_dtype, tn_pref, scale=1.0, scaled_cols=0, n=None):
    s, d = xb.shape
    n = w.shape[1] if n is None else n
    tm = _pick(s, 1024, SUBLANES)
    tn = _pick(n, tn_pref, LANES)
    if scaled_cols:
        tn = math.gcd(tn, scaled_cols)
    return pl.pallas_call(
        functools.partial(_proj_body, scale=scale, scaled_blocks=scaled_cols // tn,
                          ncols=_pick(tn, 256, LANES) if w.dtype == F32 else tn),
        grid=(s // tm, n // tn),
        in_specs=[pl.BlockSpec((tm, d), lambda i, j: (i, 0)),
                  pl.BlockSpec((d, tn), lambda i, j: (0, j))],
        out_specs=pl.BlockSpec((tm, tn), lambda i, j: (i, j)),
        out_shape=jax.ShapeDtypeStruct((s, n), out_dtype),
        compiler_params=_cparams("parallel", "arbitrary",
                                 vmem_limit_bytes=V7X_VMEM_LIMIT_F32_WEIGHTS_BYTES
                                 if w.dtype == F32 else V7X_VMEM_LIMIT_BYTES),
        name="proj",
    )(xb, w)


def _gates_body(f_ref, b_ref, o_ref, carry_ref, *, heads):
    i = pl.program_id(0)
    tm = f_ref.shape[0]

    @pl.when(i == 0)
    def _():
        carry_ref[...] = jnp.zeros_like(carry_ref)

    lane = lax.broadcasted_iota(jnp.int32, (tm, LANES), 1)
    xv = f_ref[...] + b_ref[...]
    logf = jnp.minimum(xv, 0.0) - jnp.log1p(jnp.exp(-jnp.abs(xv)))
    logf = jnp.where(lane < heads, logf, 0.0)
    row = lax.broadcasted_iota(jnp.int32, (tm, tm), 0)
    col = lax.broadcasted_iota(jnp.int32, (tm, tm), 1)
    tril = jnp.where(row >= col, 1.0, 0.0).astype(BF16)
    hi, mid, lo = _split3(logf)
    c = (jnp.dot(tril, hi, preferred_element_type=F32)
         + jnp.dot(tril, mid, preferred_element_type=F32)
         + jnp.dot(tril, lo, preferred_element_type=F32)) + carry_ref[...]
    carry_ref[...] = c[tm - 1:tm, :]
    nhi, nmid, nlo = _split3(-LOG2E * c)
    packed = jnp.where(lane < GATE_LANE_STRIDE, nhi.astype(F32),
                       jnp.where(lane < 2 * GATE_LANE_STRIDE,
                                 pltpu.roll(nmid.astype(F32), GATE_LANE_STRIDE, 1),
                                 pltpu.roll(nlo.astype(F32), 2 * GATE_LANE_STRIDE, 1)))
    o_ref[...] = packed.astype(BF16)


def _fox_gates(uf, ucols, b_f_pad, heads):
    s = uf.shape[0]
    tm = _pick(s, 512, SUBLANES)
    fblk = ucols // LANES
    return pl.pallas_call(
        functools.partial(_gates_body, heads=heads),
        grid=(s // tm,),
        in_specs=[pl.BlockSpec((tm, LANES), lambda i: (i, fblk)),
                  pl.BlockSpec((1, LANES), lambda i: (0, 0))],
        out_specs=pl.BlockSpec((tm, LANES), lambda i: (i, 0)),
        out_shape=jax.ShapeDtypeStruct((s, LANES), BF16),
        scratch_shapes=[pltpu.VMEM((1, LANES), F32)],
        compiler_params=_cparams("arbitrary"),
        name="fox_gates",
    )(uf, b_f_pad)


def _flash_body(*refs, tb, fox, rchunk):
    n_in = 4 if fox else 5
    q1_ref = refs[0]
    q2_ref = None if fox else refs[1]
    k1_ref, k2_ref, v_ref = refs[n_in - 3:n_in]
    o_ref = refs[n_in]
    q_sc, s0, s1, p0, p1, a0, a1, m_sc, acc_sc = refs[n_in + 1:]
    s_bufs, p_bufs, a_bufs = (s0, s1), (p0, p1), (a0, a1)
    h = pl.program_id(0)
    qi = pl.program_id(1)

    q_sc[:, pl.ds(0, HEAD_DIM)] = q1_ref[...]
    if fox:
        lane = lax.broadcasted_iota(jnp.int32, (tb, LANES), 1)
        sel = (lane < 3 * GATE_LANE_STRIDE) & (jnp.bitwise_and(lane, GATE_LANE_STRIDE - 1) == h)
        q_sc[:, pl.ds(HEAD_DIM, LANES)] = jnp.where(sel, 1.0, 0.0).astype(BF16)
    else:
        q_sc[:, pl.ds(HEAD_DIM, LANES)] = q2_ref[...]
    m_sc[...] = jnp.full(m_sc.shape, -jnp.inf, F32)
    acc_sc[...] = jnp.zeros(acc_sc.shape, F32)
    p1[...] = jnp.zeros(p1.shape, BF16)
    a1[...] = jnp.ones(a1.shape, F32)
    ones = jnp.ones((tb, LANES), BF16)
    hb = tb // 2

    def qk(b, s_dst):
        k0 = pl.multiple_of(b * tb, tb)
        kk = jnp.concatenate([k1_ref[pl.ds(k0, tb), :], k2_ref[pl.ds(k0, tb), :]], axis=1)
        s_dst[...] = lax.dot_general(q_sc[...], kk, (((1,), (1,)), ((), ())),
                                     preferred_element_type=F32)

    def pv(b, p_src, a_src):
        k0 = pl.multiple_of(jnp.maximum(b, 0) * tb, tb)
        vv = jnp.concatenate([v_ref[pl.ds(k0, tb), :], ones], axis=1)
        acc_sc[...] = a_src[...] * acc_sc[...] + jnp.dot(p_src[...], vv, preferred_element_type=F32)

    def sm(s_src, p_dst, a_dst, diagonal):
        for r0 in range(0, tb, rchunk):
            ncol = hb if diagonal and r0 + rchunk <= hb else tb
            s = s_src[pl.ds(r0, rchunk), pl.ds(0, ncol)]
            if diagonal:
                row = r0 + lax.broadcasted_iota(jnp.int32, (rchunk, ncol), 0)
                col = lax.broadcasted_iota(jnp.int32, (rchunk, ncol), 1)
                s = jnp.where(row >= col, s, -jnp.inf)
            m_prev = m_sc[pl.ds(r0, rchunk), :]
            m_new = jnp.maximum(m_prev, jnp.max(s, axis=1, keepdims=True))
            a_dst[pl.ds(r0, rchunk), :] = jnp.exp2(m_prev - m_new)
            p_dst[pl.ds(r0, rchunk), pl.ds(0, ncol)] = jnp.exp2(s - m_new).astype(BF16)
            m_sc[pl.ds(r0, rchunk), :] = m_new

    def pv_diagonal(b, p_src, a_src):
        k0 = pl.multiple_of(b * tb, tb)
        vv = jnp.concatenate([v_ref[pl.ds(k0, tb), :], ones], axis=1)
        acc_sc[pl.ds(0, hb), :] = (a_src[pl.ds(0, hb), :] * acc_sc[pl.ds(0, hb), :]
                                   + jnp.dot(p_src[pl.ds(0, hb), pl.ds(0, hb)], vv[:hb],
                                             preferred_element_type=F32))
        acc_sc[pl.ds(hb, hb), :] = (a_src[pl.ds(hb, hb), :] * acc_sc[pl.ds(hb, hb), :]
                                    + jnp.dot(p_src[pl.ds(hb, hb), :], vv,
                                              preferred_element_type=F32))

    def half(b, cur):
        nxt = 1 - cur
        pv(b - 1, p_bufs[nxt], a_bufs[nxt])
        qk(b + 1, s_bufs[nxt])
        sm(s_bufs[cur], p_bufs[cur], a_bufs[cur], False)

    def last(cur):
        pv(qi - 1, p_bufs[1 - cur], a_bufs[1 - cur])
        sm(s_bufs[cur], p_bufs[cur], a_bufs[cur], True)
        pv_diagonal(qi, p_bufs[cur], a_bufs[cur])
        o_ref[...] = (acc_sc[:, pl.ds(0, HEAD_DIM)]
                      / acc_sc[:, pl.ds(HEAD_DIM, LANES)]).astype(o_ref.dtype)

    qk(0, s0)

    def pair(i, carry):
        half(2 * i, 0)
        half(2 * i + 1, 1)
        return carry

    lax.fori_loop(0, lax.shift_right_logical(qi, 1), pair, 0)
    odd = jnp.bitwise_and(qi, 1)

    @pl.when(odd == 1)
    def _():
        half(qi - 1, 0)
        last(1)

    @pl.when(odd == 0)
    def _():
        last(0)


def _flash(q, q1_blk, q2_blk, kv, k1_blk, v_blk, k2, heads, tb_pref=FLASH_BLOCK):
    s = q.shape[0]
    tb = _pick(s, tb_pref, LANES)
    rchunk = _pick(tb, 32, SUBLANES)
    assert tb % (2 * LANES) == 0 and (tb // 2) % rchunk == 0
    fox = q2_blk is None
    in_specs = [pl.BlockSpec((tb, HEAD_DIM), lambda h, i: (i, q1_blk + h))]
    args = [q]
    if not fox:
        in_specs.append(pl.BlockSpec((tb, HEAD_DIM), lambda h, i: (i, q2_blk + h)))
        args.append(q)
    in_specs += [pl.BlockSpec((s, HEAD_DIM), lambda h, i: (0, k1_blk + h)),
                 pl.BlockSpec((s, LANES), lambda h, i: (0, 0)),
                 pl.BlockSpec((s, HEAD_DIM), lambda h, i: (0, v_blk + h))]
    args += [kv, k2, kv]
    score = pltpu.VMEM((tb, tb), F32)
    prob = pltpu.VMEM((tb, tb), BF16)
    col = pltpu.VMEM((tb, 1), F32)
    return pl.pallas_call(
        functools.partial(_flash_body, tb=tb, fox=fox, rchunk=rchunk),
        grid=(heads, s // tb),
        in_specs=in_specs,
        out_specs=pl.BlockSpec((tb, HEAD_DIM), lambda h, i: (i, h)),
        out_shape=jax.ShapeDtypeStruct((s, heads * HEAD_DIM), BF16),
        scratch_shapes=[pltpu.VMEM((tb, HEAD_DIM + LANES), BF16), score, score, prob, prob,
                        col, col, col, pltpu.VMEM((tb, HEAD_DIM + LANES), F32)],
        compiler_params=_cparams("parallel", "arbitrary"),
        name="flash_fox" if fox else "flash_mla",
    )(*args)


def _pool_body(u_ref, halo_ref, w_ref, sc_ref, o_ref, buf, *, groups, gdim):
    i = pl.program_id(0)
    tm = u_ref.shape[0]
    buf[pl.ds(0, POOL_HALO), :] = jnp.where(i == 0, 0.0, halo_ref[...])
    buf[pl.ds(POOL_HALO, tm), :] = u_ref[...]
    t = (i * tm + lax.broadcasted_iota(jnp.int32, (tm, 1), 0)).astype(F32)
    for g in range(groups):
        win = POOL_WINDOWS[g]
        c0 = g * gdim
        tot = buf[pl.ds(POOL_HALO, tm), pl.ds(c0, gdim)]
        cur = tot
        for lag in range(1, win):
            tot = tot + buf[pl.ds(POOL_HALO - lag, tm), pl.ds(c0, gdim)]
        count = jnp.minimum(t + 1.0, float(win))
        pooled = tot / count - cur
        mixed = jnp.dot(pooled.astype(BF16), w_ref[g], preferred_element_type=F32)
        o_ref[:, pl.ds(c0, gdim)] = (mixed * sc_ref[:, pl.ds(c0, gdim)]).astype(o_ref.dtype)


def _pool(uf, pool_w_b, pool_scale):
    s = uf.shape[0]
    groups, gdim, _ = pool_w_b.shape
    c = groups * gdim
    tm = _pick(s, 256, POOL_HALO)
    hb = tm // POOL_HALO
    return pl.pallas_call(
        functools.partial(_pool_body, groups=groups, gdim=gdim),
        grid=(s // tm,),
        in_specs=[pl.BlockSpec((tm, c), lambda i: (i, 0)),
                  pl.BlockSpec((POOL_HALO, c), lambda i: (jnp.maximum(i * hb - 1, 0), 0)),
                  pl.BlockSpec((groups, gdim, gdim), lambda i: (0, 0, 0)),
                  pl.BlockSpec((1, c), lambda i: (0, 0))],
        out_specs=pl.BlockSpec((tm, c), lambda i: (i, 0)),
        out_shape=jax.ShapeDtypeStruct((s, c), BF16),
        scratch_shapes=[pltpu.VMEM((tm + POOL_HALO, c), F32)],
        compiler_params=_cparams("parallel"),
        name="multiscale_pool",
    )(uf, uf, pool_w_b, pool_scale.reshape(1, c))


def _conv_body(a_ref, g_ref, ha_ref, hg_ref, w_ref, cb_ref, lg_ref, lb_ref, o_ref, buf, sh, ybuf, *, rows, lanes):
    i = pl.program_id(0)
    tm, c = a_ref.shape
    halo = ha_ref[...] * jax.nn.sigmoid(hg_ref[...])
    buf[pl.ds(0, CONV_HALO), :] = jnp.where(i == 0, 0.0, halo)
    buf[pl.ds(CONV_HALO, tm), :] = a_ref[...] * jax.nn.sigmoid(g_ref[...])
    span = sh.shape[1]
    for b in range(1, SUBLANES):
        sh[b - 1] = buf[pl.ds(b, span), :]
    base = CONV_HALO - (CONV_TAPS - 1)
    for r0 in range(0, tm, rows):
        for c0 in range(0, c, lanes):
            acc = jnp.zeros((rows, lanes), F32) + cb_ref[:, pl.ds(c0, lanes)]
            for j in range(CONV_TAPS):
                hi, lo = divmod(base + j, SUBLANES)
                if lo == 0:
                    win = buf[pl.ds(r0 + SUBLANES * hi, rows), pl.ds(c0, lanes)]
                else:
                    win = sh[lo - 1, pl.ds(r0 + SUBLANES * hi, rows), pl.ds(c0, lanes)]
                acc = acc + w_ref[pl.ds(j, 1), pl.ds(c0, lanes)] * win
            ybuf[pl.ds(r0, rows), pl.ds(c0, lanes)] = acc
    y = ybuf[...]
    mu = jnp.mean(y, axis=-1, keepdims=True)
    yc = y - mu
    var = jnp.mean(yc * yc, axis=-1, keepdims=True)
    z = yc * lax.rsqrt(var + LN_EPS) * lg_ref[...] + lb_ref[...]
    o_ref[...] = (z * jax.nn.sigmoid(z)).astype(o_ref.dtype)


def _conformer_conv(h1, cw, conv_w_p, conv_b, ln_g, ln_b):
    s = h1.shape[0]
    tm = _pick(s, 128, CONV_HALO)
    hb = tm // CONV_HALO
    rows = _pick(tm, 32, SUBLANES)
    lanes = _pick(cw, 256, LANES)
    nblk = 1
    vec = lambda a: a.reshape(1, cw)
    return pl.pallas_call(
        functools.partial(_conv_body, rows=rows, lanes=lanes),
        grid=(s // tm,),
        in_specs=[pl.BlockSpec((tm, cw), lambda i: (i, 0)),
                  pl.BlockSpec((tm, cw), lambda i: (i, nblk)),
                  pl.BlockSpec((CONV_HALO, cw), lambda i: (jnp.maximum(i * hb - 1, 0), 0)),
                  pl.BlockSpec((CONV_HALO, cw), lambda i: (jnp.maximum(i * hb - 1, 0), nblk)),
                  pl.BlockSpec((CONV_HALO, cw), lambda i: (0, 0)),
                  pl.BlockSpec((1, cw), lambda i: (0, 0)),
                  pl.BlockSpec((1, cw), lambda i: (0, 0)),
                  pl.BlockSpec((1, cw), lambda i: (0, 0))],
        out_specs=pl.BlockSpec((tm, cw), lambda i: (i, 0)),
        out_shape=jax.ShapeDtypeStruct((s, cw), BF16),
        scratch_shapes=[pltpu.VMEM((tm + CONV_HALO, cw), F32),
                        pltpu.VMEM((SUBLANES - 1, tm + CONV_HALO - SUBLANES, cw), F32),
                        pltpu.VMEM((tm, cw), F32)],
        compiler_params=_cparams("parallel"),
        name="conformer_conv",
    )(h1, h1, h1, h1, conv_w_p, vec(conv_b), vec(ln_g), vec(ln_b))


def _rope128(r, cosv, sinv, lane):
    half = MLA_ROPE_DIM // 2
    s1 = jnp.where(lane < half, -sinv, 0.0)
    s2 = jnp.where((lane >= half) & (lane < 2 * half), sinv, 0.0)
    return r * cosv + pltpu.roll(r, LANES - half, 1) * s1 + pltpu.roll(r, half, 1) * s2


def _rope_k_body(pos_ref, invf_ref, kpe_ref, cos_ref, sin_ref, k2_ref):
    tm = pos_ref.shape[0]
    ang = pos_ref[...].astype(F32) * invf_ref[...]
    cosv = jnp.cos(ang)
    sinv = jnp.sin(ang)
    cos_ref[...] = cosv
    sin_ref[...] = sinv
    lane = lax.broadcasted_iota(jnp.int32, (tm, LANES), 1)
    k2_ref[...] = _rope128(kpe_ref[...], cosv, sinv, lane).astype(BF16)


def _rope_k(positions_col, invf, h1, kpe_block):
    s = h1.shape[0]
    tm = _pick(s, 512, SUBLANES)
    tab = jax.ShapeDtypeStruct((s, LANES), F32)
    return pl.pallas_call(
        _rope_k_body,
        grid=(s // tm,),
        in_specs=[pl.BlockSpec((tm, 1), lambda i: (i, 0)),
                  pl.BlockSpec((1, LANES), lambda i: (0, 0)),
                  pl.BlockSpec((tm, LANES), lambda i: (i, kpe_block))],
        out_specs=[pl.BlockSpec((tm, LANES), lambda i: (i, 0))] * 3,
        out_shape=[tab, tab, jax.ShapeDtypeStruct((s, LANES), BF16)],
        compiler_params=_cparams("parallel"),
        name="rope_tables_key",
    )(positions_col, invf, h1)


def _rms_proj_body(c_ref, g_ref, w_ref, cos_ref, sin_ref, o_ref, xn_ref, *, scale, rope_from):
    j = pl.program_id(1)
    tm, tn = o_ref.shape

    @pl.when(j == 0)
    def _():
        xf = c_ref[...]
        ms = jnp.mean(xf * xf, axis=-1, keepdims=True)
        xn_ref[...] = (xf * lax.rsqrt(ms + RMS_EPS) * g_ref[...]).astype(BF16)

    acc = jnp.dot(xn_ref[...], w_ref[...], preferred_element_type=F32)

    def plain():
        o_ref[...] = (acc * scale).astype(o_ref.dtype)

    if rope_from is None:
        plain()
        return

    pl.when(j < rope_from)(plain)

    @pl.when(j >= rope_from)
    def _():
        lane = lax.broadcasted_iota(jnp.int32, (tm, LANES), 1)
        cosv = cos_ref[...]
        sinv = sin_ref[...]
        for c0 in range(0, tn, LANES):
            r = _rope128(acc[:, c0:c0 + LANES], cosv, sinv, lane)
            o_ref[:, pl.ds(c0, LANES)] = (r * scale).astype(o_ref.dtype)


def _rms_proj(h1, cblock, rank, g, w, cosv, sinv, scale, rope_cols):
    s = h1.shape[0]
    n = w.shape[1]
    tm = _pick(s, 1024, SUBLANES)
    tn = _pick(n, 1024, LANES)
    if rope_cols:
        tn = math.gcd(tn, n - rope_cols)
    rope_from = (n - rope_cols) // tn if rope_cols else None
    return pl.pallas_call(
        functools.partial(_rms_proj_body, scale=scale, rope_from=rope_from),
        grid=(s // tm, n // tn),
        in_specs=[pl.BlockSpec((tm, rank), lambda i, j: (i, cblock)),
                  pl.BlockSpec((1, rank), lambda i, j: (0, 0)),
                  pl.BlockSpec((rank, tn), lambda i, j: (0, j)),
                  pl.BlockSpec((tm, LANES), lambda i, j: (i, 0)),
                  pl.BlockSpec((tm, LANES), lambda i, j: (i, 0))],
        out_specs=pl.BlockSpec((tm, tn), lambda i, j: (i, j)),
        out_shape=jax.ShapeDtypeStruct((s, n), BF16),
        scratch_shapes=[pltpu.VMEM((tm, rank), BF16)],
        compiler_params=_cparams("parallel", "arbitrary"),
        name="rms_up_proj",
    )(h1, g.reshape(1, rank), w, cosv, sinv)


def _ffn(x32, x16, w1, w3, w2, g, b, emit_bf16=True):
    d, f = w1.shape
    fp = -(-f // 512) * 512
    w2p = _to_bf16(w2, fp)
    hidden = _ffn_gateup(x16, w1, w3, fp)
    return _proj_ln((hidden,), w2p, x32, g, b, 0.5, emit_bf16)


def _mixer_fox_pool(x32, x16, w_in, b_f, pool_w, pool_scale, w_out, g, b):
    d = x32.shape[1]
    heads = b_f.shape[0]
    fw = heads * HEAD_DIM
    groups, gdim, _ = pool_w.shape
    pw = groups * gdim
    w_uf = jnp.concatenate([w_in[:, 3 * fw + heads:], w_in[:, 3 * fw:3 * fw + heads],
                            jnp.zeros((d, LANES - heads), F32)], axis=1).astype(BF16)
    b_f_pad = jnp.pad(b_f, (0, LANES - heads)).reshape(1, LANES)

    qkv = _proj(x16, w_in.astype(BF16), BF16, 1024, scale=LOG2E * HEAD_DIM ** -0.5, scaled_cols=fw,
                n=3 * fw)
    uf = _proj(x16, w_uf, F32, pw + LANES)
    gate_parts = _fox_gates(uf, pw, b_f_pad, heads)
    y_a = _flash(qkv, 0, None, qkv, heads, 2 * heads, gate_parts, heads)
    y_b = _pool(uf, pool_w.astype(BF16), pool_scale)
    return _proj_ln((y_a, y_b), w_out.astype(BF16), x32, g, b, 1.0)


def _mixer_conv_mla(x32, x16, positions, w_in, conv_w, conv_b, conv_ln_g, conv_ln_b,
                    q_norm_g, w_uq, kv_norm_g, w_ukv, w_out, g, b):
    s, d = x32.shape
    cw = conv_b.shape[0]
    q_rank = q_norm_g.shape[0]
    kv_rank = kv_norm_g.shape[0]
    qk_dim = HEAD_DIM + MLA_ROPE_DIM
    heads = w_uq.shape[1] // qk_dim
    hw = heads * HEAD_DIM
    n_in = w_in.shape[1]
    n_in_p = -(-(n_in + MLA_ROPE_DIM) // 768) * 768
    w_in_p = jnp.pad(w_in.astype(BF16), ((0, 0), (0, n_in_p - n_in)))
    c_q_off = 2 * cw
    c_kv_off = c_q_off + q_rank
    kpe_off = c_kv_off + kv_rank
    assert c_q_off % q_rank == 0 and c_kv_off % kv_rank == 0 and kpe_off % LANES == 0

    wq = w_uq.reshape(q_rank, heads, qk_dim)
    wq_rope = jnp.pad(wq[:, :, HEAD_DIM:], ((0, 0), (0, 0), (0, LANES - MLA_ROPE_DIM)))
    wq_p = jnp.concatenate([wq[:, :, :HEAD_DIM].reshape(q_rank, hw),
                            wq_rope.reshape(q_rank, heads * LANES)], axis=1).astype(BF16)
    wkv = w_ukv.reshape(kv_rank, heads, 2 * HEAD_DIM)
    wkv_p = jnp.concatenate([wkv[:, :, :HEAD_DIM].reshape(kv_rank, hw),
                             wkv[:, :, HEAD_DIM:].reshape(kv_rank, hw)], axis=1).astype(BF16)
    conv_w_p = jnp.pad(conv_w.reshape(CONV_TAPS, cw), ((0, CONV_HALO - CONV_TAPS), (0, 0)))
    half = MLA_ROPE_DIM // 2
    invf = np.zeros((1, LANES), np.float32)
    freqs = ROPE_BASE ** (-np.arange(half, dtype=np.float32) / half)
    invf[0, :half] = freqs
    invf[0, half:2 * half] = freqs
    invf = jnp.asarray(invf)

    h1 = _proj(x16, w_in_p, F32, 768)
    y_c = _conformer_conv(h1, cw, conv_w_p, conv_b, conv_ln_g, conv_ln_b)
    cosv, sinv, k2 = _rope_k(positions.reshape(s, 1), invf, h1, kpe_off // LANES)
    q = _rms_proj(h1, c_q_off // q_rank, q_rank, q_norm_g, wq_p, cosv, sinv,
                  LOG2E * qk_dim ** -0.5, heads * LANES)
    kv = _rms_proj(h1, c_kv_off // kv_rank, kv_rank, kv_norm_g, wkv_p, cosv, sinv, 1.0, 0)
    y_d = _flash(q, 0, heads, kv, 0, heads, k2, heads)
    return _proj_ln((y_c, y_d), w_out.astype(BF16), x32, g, b, 1.0)


def kernel(x, positions, l0_ffn1_w1, l0_ffn1_w3, l0_ffn1_w2, l0_ln_ffn1_g, l0_ln_ffn1_b, l0_w_in, l0_b_f, l0_pool_w, l0_pool_scale, l0_w_out, l0_ln_mix_g, l0_ln_mix_b, l0_ffn2_w1, l0_ffn2_w3, l0_ffn2_w2, l0_ln_ffn2_g, l0_ln_ffn2_b, l1_ffn1_w1, l1_ffn1_w3, l1_ffn1_w2, l1_ln_ffn1_g, l1_ln_ffn1_b, l1_w_in, l1_conv_w, l1_conv_b, l1_conv_ln_g, l1_conv_ln_b, l1_q_norm_g, l1_w_uq, l1_kv_norm_g, l1_w_ukv, l1_w_out, l1_ln_mix_g, l1_ln_mix_b, l1_ffn2_w1, l1_ffn2_w3, l1_ffn2_w2, l1_ln_ffn2_g, l1_ln_ffn2_b):
    bsz, s, d = x.shape
    outs = []
    for bi in range(bsz):
        x32 = x.reshape(s, d) if bsz == 1 else x[bi]
        x16 = _to_bf16(x32)
        pos = positions.reshape(s) if bsz == 1 else positions[bi]
        x32, x16 = _ffn(x32, x16, l0_ffn1_w1, l0_ffn1_w3, l0_ffn1_w2, l0_ln_ffn1_g, l0_ln_ffn1_b)
        x32, x16 = _mixer_fox_pool(x32, x16, l0_w_in, l0_b_f, l0_pool_w, l0_pool_scale, l0_w_out,
                                   l0_ln_mix_g, l0_ln_mix_b)
        x32, x16 = _ffn(x32, x16, l0_ffn2_w1, l0_ffn2_w3, l0_ffn2_w2, l0_ln_ffn2_g, l0_ln_ffn2_b)
        x32, x16 = _ffn(x32, x16, l1_ffn1_w1, l1_ffn1_w3, l1_ffn1_w2, l1_ln_ffn1_g, l1_ln_ffn1_b)
        x32, x16 = _mixer_conv_mla(x32, x16, pos, l1_w_in, l1_conv_w, l1_conv_b, l1_conv_ln_g,
                                   l1_conv_ln_b, l1_q_norm_g, l1_w_uq, l1_kv_norm_g, l1_w_ukv,
                                   l1_w_out, l1_ln_mix_g, l1_ln_mix_b)
        x32, _ = _ffn(x32, x16, l1_ffn2_w1, l1_ffn2_w3, l1_ffn2_w2, l1_ln_ffn2_g, l1_ln_ffn2_b,
                      emit_bf16=False)
        outs.append(x32)
    return outs[0].reshape(1, s, d) if bsz == 1 else jnp.stack(outs, axis=0)
```

```python
import functools
import math

import numpy as np
import jax
import jax.numpy as jnp
from jax import lax
from jax.experimental import pallas as pl
from jax.experimental.pallas import tpu as pltpu

F32 = jnp.float32
BF16 = jnp.bfloat16

DEPTH = 2
ALPHA = (2 * DEPTH) ** 0.25
LN_EPS = 1e-5
RMS_EPS = 1e-6
HEAD_DIM = 128
MLA_ROPE_DIM = 64
ROPE_BASE = 10000.0
POOL_WINDOWS = (2, 4, 8, 16)
CONV_TAPS = 31

V7X_VMEM_LIMIT_BYTES = 56 * 1024 * 1024
V7X_VMEM_LIMIT_F32_WEIGHTS_BYTES = 60 * 1024 * 1024
LANES = 128
SUBLANES = 8
POOL_HALO = 16
CONV_HALO = 32
GATE_LANE_STRIDE = 32
FLASH_BLOCK = 1024
LN_ROWS = 128
LOG2E = math.log2(math.e)


def _pick(n, pref, align):
    best = None
    for d in range(align, min(n, pref) + 1, align):
        if n % d == 0:
            best = d
    return n if best is None else best


def _cparams(*sem, vmem_limit_bytes=V7X_VMEM_LIMIT_BYTES):
    return pltpu.CompilerParams(dimension_semantics=sem, vmem_limit_bytes=vmem_limit_bytes)


def _split3(x):
    hi = x.astype(BF16)
    r1 = x - hi.astype(F32)
    mid = r1.astype(BF16)
    lo = (r1 - mid.astype(F32)).astype(BF16)
    return hi, mid, lo


def _cast_body(x_ref, o_ref, *, rows_valid):
    tm = x_ref.shape[0]
    row = pl.program_id(0) * tm + lax.broadcasted_iota(jnp.int32, x_ref.shape, 0)
    o_ref[...] = jnp.where(row < rows_valid, x_ref[...], 0.0).astype(o_ref.dtype)


def _to_bf16(x, rows_out=None):
    s, d = x.shape
    rows_out = s if rows_out is None else rows_out
    tm = _pick(rows_out, 512, SUBLANES)
    return pl.pallas_call(
        functools.partial(_cast_body, rows_valid=s),
        grid=(rows_out // tm,),
        in_specs=[pl.BlockSpec((tm, d), lambda i: (i, 0))],
        out_specs=pl.BlockSpec((tm, d), lambda i: (i, 0)),
        out_shape=jax.ShapeDtypeStruct((rows_out, d), BF16),
        compiler_params=_cparams("parallel"),
        name="cast_bf16",
    )(x)


def _gateup_body(x_ref, w1_ref, w3_ref, o_ref, *, f, ncols):
    tm, tn = o_ref.shape
    x = x_ref[...]
    for c0 in range(0, tn, ncols):
        a = jnp.dot(x, w1_ref[:, pl.ds(c0, ncols)].astype(BF16), preferred_element_type=F32)
        b = jnp.dot(x, w3_ref[:, pl.ds(c0, ncols)].astype(BF16), preferred_element_type=F32)
        hidden = a * jax.nn.sigmoid(a) * b
        col = pl.program_id(1) * tn + c0 + lax.broadcasted_iota(jnp.int32, (tm, ncols), 1)
        o_ref[:, pl.ds(c0, ncols)] = jnp.where(col < f, hidden, 0.0).astype(o_ref.dtype)


def _ffn_gateup(xb, w1, w3, fp):
    s, d = xb.shape
    f = w1.shape[1]
    tm = _pick(s, 1024, SUBLANES)
    tn = _pick(fp, 512, LANES)
    return pl.pallas_call(
        functools.partial(_gateup_body, f=f, ncols=_pick(tn, 256, LANES)),
        grid=(s // tm, fp // tn),
        in_specs=[pl.BlockSpec((tm, d), lambda i, j: (i, 0)),
                  pl.BlockSpec((d, tn), lambda i, j: (0, j)),
                  pl.BlockSpec((d, tn), lambda i, j: (0, j))],
        out_specs=pl.BlockSpec((tm, tn), lambda i, j: (i, j)),
        out_shape=jax.ShapeDtypeStruct((s, fp), BF16),
        compiler_params=_cparams("parallel", "arbitrary",
                                 vmem_limit_bytes=V7X_VMEM_LIMIT_F32_WEIGHTS_BYTES),
        name="ffn_gateup",
    )(xb, w1, w3)


def _proj_ln_body(*refs, r, nk, nk_first, rows, ncols, emit_bf16):
    two_lhs = nk_first < nk
    ya_ref = refs[0]
    yb_ref = refs[1] if two_lhs else None
    w_ref, x_ref, g_ref, b_ref, o32_ref = refs[1 + two_lhs:6 + two_lhs]
    o16_ref = refs[6 + two_lhs] if emit_bf16 else None
    k = pl.program_id(1)

    @pl.when(k == 0)
    def _():
        o32_ref[...] = jnp.zeros(o32_ref.shape, F32)

    y = ya_ref[...]
    if two_lhs:
        y = jnp.where(k < nk_first, y, yb_ref[...])
    d = o32_ref.shape[1]
    for c0 in range(0, d, ncols):
        o32_ref[:, pl.ds(c0, ncols)] += jnp.dot(y, w_ref[:, pl.ds(c0, ncols)],
                                                preferred_element_type=F32)

    @pl.when(k == nk - 1)
    def _():
        g = g_ref[...]
        b = b_ref[...]
        tm = o32_ref.shape[0]

        def chunk(c, carry):
            r0 = pl.multiple_of(c * rows, rows)
            z = ALPHA * x_ref[pl.ds(r0, rows), :] + r * o32_ref[pl.ds(r0, rows), :]
            mu = jnp.mean(z, axis=-1, keepdims=True)
            zc = z - mu
            var = jnp.mean(zc * zc, axis=-1, keepdims=True)
            out = zc * lax.rsqrt(var + LN_EPS) * g + b
            o32_ref[pl.ds(r0, rows), :] = out
            if emit_bf16:
                o16_ref[pl.ds(r0, rows), :] = out.astype(BF16)
            return carry

        lax.fori_loop(0, tm // rows, chunk, 0)


def _proj_ln(ys, w, x, g, b, r, emit_bf16=True):
    s = ys[0].shape[0]
    kdim, d = w.shape
    tm = _pick(s, 512, SUBLANES)
    tk = _pick(math.gcd(*[y.shape[1] for y in ys]), 512, LANES)
    nk = kdim // tk
    nk_first = ys[0].shape[1] // tk
    y_specs = [pl.BlockSpec((tm, tk), lambda i, k: (i, jnp.minimum(k, nk_first - 1)))]
    if len(ys) == 2:
        y_specs.append(pl.BlockSpec((tm, tk), lambda i, k: (i, jnp.maximum(k - nk_first, 0))))
    rows = _pick(tm, LN_ROWS, SUBLANES)
    ncols = _pick(d, 1024, LANES)
    out_shape = [jax.ShapeDtypeStruct((s, d), F32)]
    out_specs = [pl.BlockSpec((tm, d), lambda i, k: (i, 0))]
    if emit_bf16:
        out_shape.append(jax.ShapeDtypeStruct((s, d), BF16))
        out_specs.append(pl.BlockSpec((tm, d), lambda i, k: (i, 0)))
    outs = pl.pallas_call(
        functools.partial(_proj_ln_body, r=r, nk=nk, nk_first=nk_first, rows=rows, ncols=ncols,
                          emit_bf16=emit_bf16),
        grid=(s // tm, nk),
        in_specs=y_specs + [pl.BlockSpec((tk, d), lambda i, k: (k, 0)),
                            pl.BlockSpec((tm, d), lambda i, k: (i, 0)),
                            pl.BlockSpec((1, d), lambda i, k: (0, 0)),
                            pl.BlockSpec((1, d), lambda i, k: (0, 0))],
        out_specs=out_specs,
        out_shape=out_shape,
        compiler_params=_cparams("parallel", "arbitrary"),
        name="proj_residual_ln",
    )(*ys, w, x, g.reshape(1, d), b.reshape(1, d))
    return outs if emit_bf16 else (outs[0], None)


def _proj_body(x_ref, w_ref, o_ref, *, scale, scaled_blocks, ncols):
    x = x_ref[...]
    tn = o_ref.shape[1]
    for c0 in range(0, tn, ncols):
        acc = jnp.dot(x, w_ref[:, pl.ds(c0, ncols)].astype(BF16), preferred_element_type=F32)
        if scaled_blocks:
            acc = acc * jnp.where(pl.program_id(1) < scaled_blocks, scale, 1.0).astype(F32)
        o_ref[:, pl.ds(c0, ncols)] = acc.astype(o_ref.dtype)


def _proj(xb, w, out_dtype, tn_pref, scale=1.0, scaled_cols=0, n=None):
    s, d = xb.shape
    n = w.shape[1] if n is None else n
    tm = _pick(s, 1024, SUBLANES)
    tn = _pick(n, tn_pref, LANES)
    if scaled_cols:
        tn = math.gcd(tn, scaled_cols)
    return pl.pallas_call(
        functools.partial(_proj_body, scale=scale, scaled_blocks=scaled_cols // tn,
                          ncols=_pick(tn, 256, LANES) if w.dtype == F32 else tn),
        grid=(s // tm, n // tn),
        in_specs=[pl.BlockSpec((tm, d), lambda i, j: (i, 0)),
                  pl.BlockSpec((d, tn), lambda i, j: (0, j))],
        out_specs=pl.BlockSpec((tm, tn), lambda i, j: (i, j)),
        out_shape=jax.ShapeDtypeStruct((s, n), out_dtype),
        compiler_params=_cparams("parallel", "arbitrary",
                                 vmem_limit_bytes=V7X_VMEM_LIMIT_F32_WEIGHTS_BYTES
                                 if w.dtype == F32 else V7X_VMEM_LIMIT_BYTES),
        name="proj",
    )(xb, w)


def _gates_body(f_ref, b_ref, o_ref, carry_ref, *, heads):
    i = pl.program_id(0)
    tm = f_ref.shape[0]

    @pl.when(i == 0)
    def _():
        carry_ref[...] = jnp.zeros_like(carry_ref)

    lane = lax.broadcasted_iota(jnp.int32, (tm, LANES), 1)
    xv = f_ref[...] + b_ref[...]
    logf = jnp.minimum(xv, 0.0) - jnp.log1p(jnp.exp(-jnp.abs(xv)))
    logf = jnp.where(lane < heads, logf, 0.0)
    row = lax.broadcasted_iota(jnp.int32, (tm, tm), 0)
    col = lax.broadcasted_iota(jnp.int32, (tm, tm), 1)
    tril = jnp.where(row >= col, 1.0, 0.0).astype(BF16)
    hi, mid, lo = _split3(logf)
    c = (jnp.dot(tril, hi, preferred_element_type=F32)
         + jnp.dot(tril, mid, preferred_element_type=F32)
         + jnp.dot(tril, lo, preferred_element_type=F32)) + carry_ref[...]
    carry_ref[...] = c[tm - 1:tm, :]
    nhi, nmid, nlo = _split3(-LOG2E * c)
    packed = jnp.where(lane < GATE_LANE_STRIDE, nhi.astype(F32),
                       jnp.where(lane < 2 * GATE_LANE_STRIDE,
                                 pltpu.roll(nmid.astype(F32), GATE_LANE_STRIDE, 1),
                                 pltpu.roll(nlo.astype(F32), 2 * GATE_LANE_STRIDE, 1)))
    o_ref[...] = packed.astype(BF16)


def _fox_gates(uf, ucols, b_f_pad, heads):
    s = uf.shape[0]
    tm = _pick(s, 512, SUBLANES)
    fblk = ucols // LANES
    return pl.pallas_call(
        functools.partial(_gates_body, heads=heads),
        grid=(s // tm,),
        in_specs=[pl.BlockSpec((tm, LANES), lambda i: (i, fblk)),
                  pl.BlockSpec((1, LANES), lambda i: (0, 0))],
        out_specs=pl.BlockSpec((tm, LANES), lambda i: (i, 0)),
        out_shape=jax.ShapeDtypeStruct((s, LANES), BF16),
        scratch_shapes=[pltpu.VMEM((1, LANES), F32)],
        compiler_params=_cparams("arbitrary"),
        name="fox_gates",
    )(uf, b_f_pad)


def _flash_body(*refs, tb, fox, rchunk):
    n_in = 4 if fox else 5
    q1_ref = refs[0]
    q2_ref = None if fox else refs[1]
    k1_ref, k2_ref, v_ref = refs[n_in - 3:n_in]
    o_ref = refs[n_in]
    q_sc, s0, s1, p0, p1, a0, a1, m_sc, acc_sc = refs[n_in + 1:]
    s_bufs, p_bufs, a_bufs = (s0, s1), (p0, p1), (a0, a1)
    h = pl.program_id(0)
    qi = pl.program_id(1)

    q_sc[:, pl.ds(0, HEAD_DIM)] = q1_ref[...]
    if fox:
        lane = lax.broadcasted_iota(jnp.int32, (tb, LANES), 1)
        sel = (lane < 3 * GATE_LANE_STRIDE) & (jnp.bitwise_and(lane, GATE_LANE_STRIDE - 1) == h)
        q_sc[:, pl.ds(HEAD_DIM, LANES)] = jnp.where(sel, 1.0, 0.0).astype(BF16)
    else:
        q_sc[:, pl.ds(HEAD_DIM, LANES)] = q2_ref[...]
    m_sc[...] = jnp.full(m_sc.shape, -jnp.inf, F32)
    acc_sc[...] = jnp.zeros(acc_sc.shape, F32)
    p1[...] = jnp.zeros(p1.shape, BF16)
    a1[...] = jnp.ones(a1.shape, F32)
    ones = jnp.ones((tb, LANES), BF16)
    hb = tb // 2

    def qk(b, s_dst):
        k0 = pl.multiple_of(b * tb, tb)
        kk = jnp.concatenate([k1_ref[pl.ds(k0, tb), :], k2_ref[pl.ds(k0, tb), :]], axis=1)
        s_dst[...] = lax.dot_general(q_sc[...], kk, (((1,), (1,)), ((), ())),
                                     preferred_element_type=F32)

    def pv(b, p_src, a_src):
        k0 = pl.multiple_of(jnp.maximum(b, 0) * tb, tb)
        vv = jnp.concatenate([v_ref[pl.ds(k0, tb), :], ones], axis=1)
        acc_sc[...] = a_src[...] * acc_sc[...] + jnp.dot(p_src[...], vv, preferred_element_type=F32)

    def sm(s_src, p_dst, a_dst, diagonal):
        for r0 in range(0, tb, rchunk):
            ncol = hb if diagonal and r0 + rchunk <= hb else tb
            s = s_src[pl.ds(r0, rchunk), pl.ds(0, ncol)]
            if diagonal:
                row = r0 + lax.broadcasted_iota(jnp.int32, (rchunk, ncol), 0)
                col = lax.broadcasted_iota(jnp.int32, (rchunk, ncol), 1)
                s = jnp.where(row >= col, s, -jnp.inf)
            m_prev = m_sc[pl.ds(r0, rchunk), :]
            m_new = jnp.maximum(m_prev, jnp.max(s, axis=1, keepdims=True))
            a_dst[pl.ds(r0, rchunk), :] = jnp.exp2(m_prev - m_new)
            p_dst[pl.ds(r0, rchunk), pl.ds(0, ncol)] = jnp.exp2(s - m_new).astype(BF16)
            m_sc[pl.ds(r0, rchunk), :] = m_new

    def pv_diagonal(b, p_src, a_src):
        k0 = pl.multiple_of(b * tb, tb)
        vv = jnp.concatenate([v_ref[pl.ds(k0, tb), :], ones], axis=1)
        acc_sc[pl.ds(0, hb), :] = (a_src[pl.ds(0, hb), :] * acc_sc[pl.ds(0, hb), :]
                                   + jnp.dot(p_src[pl.ds(0, hb), pl.ds(0, hb)], vv[:hb],
                                             preferred_element_type=F32))
        acc_sc[pl.ds(hb, hb), :] = (a_src[pl.ds(hb, hb), :] * acc_sc[pl.ds(hb, hb), :]
                                    + jnp.dot(p_src[pl.ds(hb, hb), :], vv,
                                              preferred_element_type=F32))

    def half(b, cur):
        nxt = 1 - cur
        pv(b - 1, p_bufs[nxt], a_bufs[nxt])
        qk(b + 1, s_bufs[nxt])
        sm(s_bufs[cur], p_bufs[cur], a_bufs[cur], False)

    def last(cur):
        pv(qi - 1, p_bufs[1 - cur], a_bufs[1 - cur])
        sm(s_bufs[cur], p_bufs[cur], a_bufs[cur], True)
        pv_diagonal(qi, p_bufs[cur], a_bufs[cur])
        o_ref[...] = (acc_sc[:, pl.ds(0, HEAD_DIM)]
                      / acc_sc[:, pl.ds(HEAD_DIM, LANES)]).astype(o_ref.dtype)

    qk(0, s0)

    def pair(i, carry):
        half(2 * i, 0)
        half(2 * i + 1, 1)
        return carry

    lax.fori_loop(0, lax.shift_right_logical(qi, 1), pair, 0)
    odd = jnp.bitwise_and(qi, 1)

    @pl.when(odd == 1)
    def _():
        half(qi - 1, 0)
        last(1)

    @pl.when(odd == 0)
    def _():
        last(0)


def _flash(q, q1_blk, q2_blk, kv, k1_blk, v_blk, k2, heads, tb_pref=FLASH_BLOCK):
    s = q.shape[0]
    tb = _pick(s, tb_pref, LANES)
    rchunk = _pick(tb, 64, SUBLANES)
    assert tb % (2 * LANES) == 0 and (tb // 2) % rchunk == 0
    fox = q2_blk is None
    in_specs = [pl.BlockSpec((tb, HEAD_DIM), lambda h, i: (i, q1_blk + h))]
    args = [q]
    if not fox:
        in_specs.append(pl.BlockSpec((tb, HEAD_DIM), lambda h, i: (i, q2_blk + h)))
        args.append(q)
    in_specs += [pl.BlockSpec((s, HEAD_DIM), lambda h, i: (0, k1_blk + h)),
                 pl.BlockSpec((s, LANES), lambda h, i: (0, 0)),
                 pl.BlockSpec((s, HEAD_DIM), lambda h, i: (0, v_blk + h))]
    args += [kv, k2, kv]
    score = pltpu.VMEM((tb, tb), F32)
    prob = pltpu.VMEM((tb, tb), BF16)
    col = pltpu.VMEM((tb, 1), F32)
    return pl.pallas_call(
        functools.partial(_flash_body, tb=tb, fox=fox, rchunk=rchunk),
        grid=(heads, s // tb),
        in_specs=in_specs,
        out_specs=pl.BlockSpec((tb, HEAD_DIM), lambda h, i: (i, h)),
        out_shape=jax.ShapeDtypeStruct((s, heads * HEAD_DIM), BF16),
        scratch_shapes=[pltpu.VMEM((tb, HEAD_DIM + LANES), BF16), score, score, prob, prob,
                        col, col, col, pltpu.VMEM((tb, HEAD_DIM + LANES), F32)],
        compiler_params=_cparams("parallel", "arbitrary"),
        name="flash_fox" if fox else "flash_mla",
    )(*args)


def _pool_body(u_ref, halo_ref, w_ref, sc_ref, o_ref, buf, *, groups, gdim):
    i = pl.program_id(0)
    tm = u_ref.shape[0]
    buf[pl.ds(0, POOL_HALO), :] = jnp.where(i == 0, 0.0, halo_ref[...])
    buf[pl.ds(POOL_HALO, tm), :] = u_ref[...]
    t = (i * tm + lax.broadcasted_iota(jnp.int32, (tm, 1), 0)).astype(F32)
    for g in range(groups):
        win = POOL_WINDOWS[g]
        c0 = g * gdim
        tot = buf[pl.ds(POOL_HALO, tm), pl.ds(c0, gdim)]
        cur = tot
        for lag in range(1, win):
            tot = tot + buf[pl.ds(POOL_HALO - lag, tm), pl.ds(c0, gdim)]
        count = jnp.minimum(t + 1.0, float(win))
        pooled = tot / count - cur
        mixed = jnp.dot(pooled.astype(BF16), w_ref[g], preferred_element_type=F32)
        o_ref[:, pl.ds(c0, gdim)] = (mixed * sc_ref[:, pl.ds(c0, gdim)]).astype(o_ref.dtype)


def _pool(uf, pool_w_b, pool_scale):
    s = uf.shape[0]
    groups, gdim, _ = pool_w_b.shape
    c = groups * gdim
    tm = _pick(s, 256, POOL_HALO)
    hb = tm // POOL_HALO
    return pl.pallas_call(
        functools.partial(_pool_body, groups=groups, gdim=gdim),
        grid=(s // tm,),
        in_specs=[pl.BlockSpec((tm, c), lambda i: (i, 0)),
                  pl.BlockSpec((POOL_HALO, c), lambda i: (jnp.maximum(i * hb - 1, 0), 0)),
                  pl.BlockSpec((groups, gdim, gdim), lambda i: (0, 0, 0)),
                  pl.BlockSpec((1, c), lambda i: (0, 0))],
        out_specs=pl.BlockSpec((tm, c), lambda i: (i, 0)),
        out_shape=jax.ShapeDtypeStruct((s, c), BF16),
        scratch_shapes=[pltpu.VMEM((tm + POOL_HALO, c), F32)],
        compiler_params=_cparams("parallel"),
        name="multiscale_pool",
    )(uf, uf, pool_w_b, pool_scale.reshape(1, c))


def _conv_body(a_ref, g_ref, ha_ref, hg_ref, w_ref, cb_ref, lg_ref, lb_ref, o_ref, buf, sh, ybuf, *, rows, lanes):
    i = pl.program_id(0)
    tm, c = a_ref.shape
    halo = ha_ref[...] * jax.nn.sigmoid(hg_ref[...])
    buf[pl.ds(0, CONV_HALO), :] = jnp.where(i == 0, 0.0, halo)
    buf[pl.ds(CONV_HALO, tm), :] = a_ref[...] * jax.nn.sigmoid(g_ref[...])
    span = sh.shape[1]
    for b in range(1, SUBLANES):
        sh[b - 1] = buf[pl.ds(b, span), :]
    base = CONV_HALO - (CONV_TAPS - 1)
    for r0 in range(0, tm, rows):
        for c0 in range(0, c, lanes):
            acc = jnp.zeros((rows, lanes), F32) + cb_ref[:, pl.ds(c0, lanes)]
            for j in range(CONV_TAPS):
                hi, lo = divmod(base + j, SUBLANES)
                if lo == 0:
                    win = buf[pl.ds(r0 + SUBLANES * hi, rows), pl.ds(c0, lanes)]
                else:
                    win = sh[lo - 1, pl.ds(r0 + SUBLANES * hi, rows), pl.ds(c0, lanes)]
                acc = acc + w_ref[pl.ds(j, 1), pl.ds(c0, lanes)] * win
            ybuf[pl.ds(r0, rows), pl.ds(c0, lanes)] = acc
    y = ybuf[...]
    mu = jnp.mean(y, axis=-1, keepdims=True)
    yc = y - mu
    var = jnp.mean(yc * yc, axis=-1, keepdims=True)
    z = yc * lax.rsqrt(var + LN_EPS) * lg_ref[...] + lb_ref[...]
    o_ref[...] = (z * jax.nn.sigmoid(z)).astype(o_ref.dtype)


def _conformer_conv(h1, cw, conv_w_p, conv_b, ln_g, ln_b):
    s = h1.shape[0]
    tm = _pick(s, 128, CONV_HALO)
    hb = tm // CONV_HALO
    rows = _pick(tm, 32, SUBLANES)
    lanes = _pick(cw, 256, LANES)
    nblk = 1
    vec = lambda a: a.reshape(1, cw)
    return pl.pallas_call(
        functools.partial(_conv_body, rows=rows, lanes=lanes),
        grid=(s // tm,),
        in_specs=[pl.BlockSpec((tm, cw), lambda i: (i, 0)),
                  pl.BlockSpec((tm, cw), lambda i: (i, nblk)),
                  pl.BlockSpec((CONV_HALO, cw), lambda i: (jnp.maximum(i * hb - 1, 0), 0)),
                  pl.BlockSpec((CONV_HALO, cw), lambda i: (jnp.maximum(i * hb - 1, 0), nblk)),
                  pl.BlockSpec((CONV_HALO, cw), lambda i: (0, 0)),
                  pl.BlockSpec((1, cw), lambda i: (0, 0)),
                  pl.BlockSpec((1, cw), lambda i: (0, 0)),
                  pl.BlockSpec((1, cw), lambda i: (0, 0))],
        out_specs=pl.BlockSpec((tm, cw), lambda i: (i, 0)),
        out_shape=jax.ShapeDtypeStruct((s, cw), BF16),
        scratch_shapes=[pltpu.VMEM((tm + CONV_HALO, cw), F32),
                        pltpu.VMEM((SUBLANES - 1, tm + CONV_HALO - SUBLANES, cw), F32),
                        pltpu.VMEM((tm, cw), F32)],
        compiler_params=_cparams("parallel"),
        name="conformer_conv",
    )(h1, h1, h1, h1, conv_w_p, vec(conv_b), vec(ln_g), vec(ln_b))


def _rope128(r, cosv, sinv, lane):
    half = MLA_ROPE_DIM // 2
    s1 = jnp.where(lane < half, -sinv, 0.0)
    s2 = jnp.where((lane >= half) & (lane < 2 * half), sinv, 0.0)
    return r * cosv + pltpu.roll(r, LANES - half, 1) * s1 + pltpu.roll(r, half, 1) * s2


def _rope_k_body(pos_ref, invf_ref, kpe_ref, cos_ref, sin_ref, k2_ref):
    tm = pos_ref.shape[0]
    ang = pos_ref[...].astype(F32) * invf_ref[...]
    cosv = jnp.cos(ang)
    sinv = jnp.sin(ang)
    cos_ref[...] = cosv
    sin_ref[...] = sinv
    lane = lax.broadcasted_iota(jnp.int32, (tm, LANES), 1)
    k2_ref[...] = _rope128(kpe_ref[...], cosv, sinv, lane).astype(BF16)


def _rope_k(positions_col, invf, h1, kpe_block):
    s = h1.shape[0]
    tm = _pick(s, 512, SUBLANES)
    tab = jax.ShapeDtypeStruct((s, LANES), F32)
    return pl.pallas_call(
        _rope_k_body,
        grid=(s // tm,),
        in_specs=[pl.BlockSpec((tm, 1), lambda i: (i, 0)),
                  pl.BlockSpec((1, LANES), lambda i: (0, 0)),
                  pl.BlockSpec((tm, LANES), lambda i: (i, kpe_block))],
        out_specs=[pl.BlockSpec((tm, LANES), lambda i: (i, 0))] * 3,
        out_shape=[tab, tab, jax.ShapeDtypeStruct((s, LANES), BF16)],
        compiler_params=_cparams("parallel"),
        name="rope_tables_key",
    )(positions_col, invf, h1)


def _rms_proj_body(c_ref, g_ref, w_ref, cos_ref, sin_ref, o_ref, xn_ref, *, scale, rope_from):
    j = pl.program_id(1)
    tm, tn = o_ref.shape

    @pl.when(j == 0)
    def _():
        xf = c_ref[...]
        ms = jnp.mean(xf * xf, axis=-1, keepdims=True)
        xn_ref[...] = (xf * lax.rsqrt(ms + RMS_EPS) * g_ref[...]).astype(BF16)

    acc = jnp.dot(xn_ref[...], w_ref[...], preferred_element_type=F32)

    def plain():
        o_ref[...] = (acc * scale).astype(o_ref.dtype)

    if rope_from is None:
        plain()
        return

    pl.when(j < rope_from)(plain)

    @pl.when(j >= rope_from)
    def _():
        lane = lax.broadcasted_iota(jnp.int32, (tm, LANES), 1)
        cosv = cos_ref[...]
        sinv = sin_ref[...]
        for c0 in range(0, tn, LANES):
            r = _rope128(acc[:, c0:c0 + LANES], cosv, sinv, lane)
            o_ref[:, pl.ds(c0, LANES)] = (r * scale).astype(o_ref.dtype)


def _rms_proj(h1, cblock, rank, g, w, cosv, sinv, scale, rope_cols):
    s = h1.shape[0]
    n = w.shape[1]
    tm = _pick(s, 1024, SUBLANES)
    tn = _pick(n, 1024, LANES)
    if rope_cols:
        tn = math.gcd(tn, n - rope_cols)
    rope_from = (n - rope_cols) // tn if rope_cols else None
    return pl.pallas_call(
        functools.partial(_rms_proj_body, scale=scale, rope_from=rope_from),
        grid=(s // tm, n // tn),
        in_specs=[pl.BlockSpec((tm, rank), lambda i, j: (i, cblock)),
                  pl.BlockSpec((1, rank), lambda i, j: (0, 0)),
                  pl.BlockSpec((rank, tn), lambda i, j: (0, j)),
                  pl.BlockSpec((tm, LANES), lambda i, j: (i, 0)),
                  pl.BlockSpec((tm, LANES), lambda i, j: (i, 0))],
        out_specs=pl.BlockSpec((tm, tn), lambda i, j: (i, j)),
        out_shape=jax.ShapeDtypeStruct((s, n), BF16),
        scratch_shapes=[pltpu.VMEM((tm, rank), BF16)],
        compiler_params=_cparams("parallel", "arbitrary"),
        name="rms_up_proj",
    )(h1, g.reshape(1, rank), w, cosv, sinv)


def _ffn(x32, x16, w1, w3, w2, g, b, emit_bf16=True):
    d, f = w1.shape
    fp = -(-f // 512) * 512
    w2p = _to_bf16(w2, fp)
    hidden = _ffn_gateup(x16, w1, w3, fp)
    return _proj_ln((hidden,), w2p, x32, g, b, 0.5, emit_bf16)


def _mixer_fox_pool(x32, x16, w_in, b_f, pool_w, pool_scale, w_out, g, b):
    d = x32.shape[1]
    heads = b_f.shape[0]
    fw = heads * HEAD_DIM
    groups, gdim, _ = pool_w.shape
    pw = groups * gdim
    w_uf = jnp.concatenate([w_in[:, 3 * fw + heads:], w_in[:, 3 * fw:3 * fw + heads],
                            jnp.zeros((d, LANES - heads), F32)], axis=1).astype(BF16)
    b_f_pad = jnp.pad(b_f, (0, LANES - heads)).reshape(1, LANES)

    qkv = _proj(x16, w_in.astype(BF16), BF16, 1024, scale=LOG2E * HEAD_DIM ** -0.5, scaled_cols=fw,
                n=3 * fw)
    uf = _proj(x16, w_uf, F32, pw + LANES)
    gate_parts = _fox_gates(uf, pw, b_f_pad, heads)
    y_a = _flash(qkv, 0, None, qkv, heads, 2 * heads, gate_parts, heads)
    y_b = _pool(uf, pool_w.astype(BF16), pool_scale)
    return _proj_ln((y_a, y_b), w_out.astype(BF16), x32, g, b, 1.0)


def _mixer_conv_mla(x32, x16, positions, w_in, conv_w, conv_b, conv_ln_g, conv_ln_b,
                    q_norm_g, w_uq, kv_norm_g, w_ukv, w_out, g, b):
    s, d = x32.shape
    cw = conv_b.shape[0]
    q_rank = q_norm_g.shape[0]
    kv_rank = kv_norm_g.shape[0]
    qk_dim = HEAD_DIM + MLA_ROPE_DIM
    heads = w_uq.shape[1] // qk_dim
    hw = heads * HEAD_DIM
    n_in = w_in.shape[1]
    n_in_p = -(-(n_in + MLA_ROPE_DIM) // 768) * 768
    w_in_p = jnp.pad(w_in.astype(BF16), ((0, 0), (0, n_in_p - n_in)))
    c_q_off = 2 * cw
    c_kv_off = c_q_off + q_rank
    kpe_off = c_kv_off + kv_rank
    assert c_q_off % q_rank == 0 and c_kv_off % kv_rank == 0 and kpe_off % LANES == 0

    wq = w_uq.reshape(q_rank, heads, qk_dim)
    wq_rope = jnp.pad(wq[:, :, HEAD_DIM:], ((0, 0), (0, 0), (0, LANES - MLA_ROPE_DIM)))
    wq_p = jnp.concatenate([wq[:, :, :HEAD_DIM].reshape(q_rank, hw),
                            wq_rope.reshape(q_rank, heads * LANES)], axis=1).astype(BF16)
    wkv = w_ukv.reshape(kv_rank, heads, 2 * HEAD_DIM)
    wkv_p = jnp.concatenate([wkv[:, :, :HEAD_DIM].reshape(kv_rank, hw),
                             wkv[:, :, HEAD_DIM:].reshape(kv_rank, hw)], axis=1).astype(BF16)
    conv_w_p = jnp.pad(conv_w.reshape(CONV_TAPS, cw), ((0, CONV_HALO - CONV_TAPS), (0, 0)))
    half = MLA_ROPE_DIM // 2
    invf = np.zeros((1, LANES), np.float32)
    freqs = ROPE_BASE ** (-np.arange(half, dtype=np.float32) / half)
    invf[0, :half] = freqs
    invf[0, half:2 * half] = freqs
    invf = jnp.asarray(invf)

    h1 = _proj(x16, w_in_p, F32, 768)
    y_c = _conformer_conv(h1, cw, conv_w_p, conv_b, conv_ln_g, conv_ln_b)
    cosv, sinv, k2 = _rope_k(positions.reshape(s, 1), invf, h1, kpe_off // LANES)
    q = _rms_proj(h1, c_q_off // q_rank, q_rank, q_norm_g, wq_p, cosv, sinv,
                  LOG2E * qk_dim ** -0.5, heads * LANES)
    kv = _rms_proj(h1, c_kv_off // kv_rank, kv_rank, kv_norm_g, wkv_p, cosv, sinv, 1.0, 0)
    y_d = _flash(q, 0, heads, kv, 0, heads, k2, heads)
    return _proj_ln((y_c, y_d), w_out.astype(BF16), x32, g, b, 1.0)


def kernel(x, positions, l0_ffn1_w1, l0_ffn1_w3, l0_ffn1_w2, l0_ln_ffn1_g, l0_ln_ffn1_b, l0_w_in, l0_b_f, l0_pool_w, l0_pool_scale, l0_w_out, l0_ln_mix_g, l0_ln_mix_b, l0_ffn2_w1, l0_ffn2_w3, l0_ffn2_w2, l0_ln_ffn2_g, l0_ln_ffn2_b, l1_ffn1_w1, l1_ffn1_w3, l1_ffn1_w2, l1_ln_ffn1_g, l1_ln_ffn1_b, l1_w_in, l1_conv_w, l1_conv_b, l1_conv_ln_g, l1_conv_ln_b, l1_q_norm_g, l1_w_uq, l1_kv_norm_g, l1_w_ukv, l1_w_out, l1_ln_mix_g, l1_ln_mix_b, l1_ffn2_w1, l1_ffn2_w3, l1_ffn2_w2, l1_ln_ffn2_g, l1_ln_ffn2_b):
    bsz, s, d = x.shape
    outs = []
    for bi in range(bsz):
        x32 = x.reshape(s, d) if bsz == 1 else x[bi]
        x16 = _to_bf16(x32)
        pos = positions.reshape(s) if bsz == 1 else positions[bi]
        x32, x16 = _ffn(x32, x16, l0_ffn1_w1, l0_ffn1_w3, l0_ffn1_w2, l0_ln_ffn1_g, l0_ln_ffn1_b)
        x32, x16 = _mixer_fox_pool(x32, x16, l0_w_in, l0_b_f, l0_pool_w, l0_pool_scale, l0_w_out,
                                   l0_ln_mix_g, l0_ln_mix_b)
        x32, x16 = _ffn(x32, x16, l0_ffn2_w1, l0_ffn2_w3, l0_ffn2_w2, l0_ln_ffn2_g, l0_ln_ffn2_b)
        x32, x16 = _ffn(x32, x16, l1_ffn1_w1, l1_ffn1_w3, l1_ffn1_w2, l1_ln_ffn1_g, l1_ln_ffn1_b)
        x32, x16 = _mixer_conv_mla(x32, x16, pos, l1_w_in, l1_conv_w, l1_conv_b, l1_conv_ln_g,
                                   l1_conv_ln_b, l1_q_norm_g, l1_w_uq, l1_kv_norm_g, l1_w_ukv,
                                   l1_w_out, l1_ln_mix_g, l1_ln_mix_b)
        x32, _ = _ffn(x32, x16, l1_ffn2_w1, l1_ffn2_w3, l1_ffn2_w2, l1_ln_ffn2_g, l1_ln_ffn2_b,
                      emit_bf16=False)
        outs.append(x32)
    return outs[0].reshape(1, s, d) if bsz == 1 else jnp.stack(outs, axis=0)
```

```python
import functools
import math

import numpy as np
import jax
import jax.numpy as jnp
from jax import lax
from jax.experimental import pallas as pl
from jax.experimental.pallas import tpu as pltpu

F32 = jnp.float32
BF16 = jnp.bfloat16

DEPTH = 2
ALPHA = (2 * DEPTH) ** 0.25
LN_EPS = 1e-5
RMS_EPS = 1e-6
HEAD_DIM = 128
MLA_ROPE_DIM = 64
ROPE_BASE = 10000.0
POOL_WINDOWS = (2, 4, 8, 16)
CONV_TAPS = 31

V7X_VMEM_LIMIT_BYTES = 56 * 1024 * 1024
V7X_VMEM_LIMIT_F32_WEIGHTS_BYTES = 60 * 1024 * 1024
LANES = 128
SUBLANES = 8
POOL_HALO = 16
CONV_HALO = 32
GATE_LANE_STRIDE = 32
FLASH_BLOCK = 1024
LN_ROWS = 128
LOG2E = math.log2(math.e)


def _pick(n, pref, align):
    best = None
    for d in range(align, min(n, pref) + 1, align):
        if n % d == 0:
            best = d
    return n if best is None else best


def _cparams(*sem, vmem_limit_bytes=V7X_VMEM_LIMIT_BYTES):
    return pltpu.CompilerParams(dimension_semantics=sem, vmem_limit_bytes=vmem_limit_bytes)


def _split3(x):
    hi = x.astype(BF16)
    r1 = x - hi.astype(F32)
    mid = r1.astype(BF16)
    lo = (r1 - mid.astype(F32)).astype(BF16)
    return hi, mid, lo


def _cast_body(x_ref, o_ref, *, rows_valid):
    tm = x_ref.shape[0]
    row = pl.program_id(0) * tm + lax.broadcasted_iota(jnp.int32, x_ref.shape, 0)
    o_ref[...] = jnp.where(row < rows_valid, x_ref[...], 0.0).astype(o_ref.dtype)


def _to_bf16(x, rows_out=None):
    s, d = x.shape
    rows_out = s if rows_out is None else rows_out
    tm = _pick(rows_out, 512, SUBLANES)
    return pl.pallas_call(
        functools.partial(_cast_body, rows_valid=s),
        grid=(rows_out // tm,),
        in_specs=[pl.BlockSpec((tm, d), lambda i: (i, 0))],
        out_specs=pl.BlockSpec((tm, d), lambda i: (i, 0)),
        out_shape=jax.ShapeDtypeStruct((rows_out, d), BF16),
        compiler_params=_cparams("parallel"),
        name="cast_bf16",
    )(x)


def _gateup_body(x_ref, w1_ref, w3_ref, o_ref, *, f, ncols):
    tm, tn = o_ref.shape
    x = x_ref[...]
    for c0 in range(0, tn, ncols):
        a = jnp.dot(x, w1_ref[:, pl.ds(c0, ncols)].astype(BF16), preferred_element_type=F32)
        b = jnp.dot(x, w3_ref[:, pl.ds(c0, ncols)].astype(BF16), preferred_element_type=F32)
        hidden = a * jax.nn.sigmoid(a) * b
        col = pl.program_id(1) * tn + c0 + lax.broadcasted_iota(jnp.int32, (tm, ncols), 1)
        o_ref[:, pl.ds(c0, ncols)] = jnp.where(col < f, hidden, 0.0).astype(o_ref.dtype)


def _ffn_gateup(xb, w1, w3, fp):
    s, d = xb.shape
    f = w1.shape[1]
    tm = _pick(s, 1024, SUBLANES)
    tn = _pick(fp, 512, LANES)
    return pl.pallas_call(
        functools.partial(_gateup_body, f=f, ncols=_pick(tn, 256, LANES)),
        grid=(s // tm, fp // tn),
        in_specs=[pl.BlockSpec((tm, d), lambda i, j: (i, 0)),
                  pl.BlockSpec((d, tn), lambda i, j: (0, j)),
                  pl.BlockSpec((d, tn), lambda i, j: (0, j))],
        out_specs=pl.BlockSpec((tm, tn), lambda i, j: (i, j)),
        out_shape=jax.ShapeDtypeStruct((s, fp), BF16),
        compiler_params=_cparams("parallel", "arbitrary",
                                 vmem_limit_bytes=V7X_VMEM_LIMIT_F32_WEIGHTS_BYTES),
        name="ffn_gateup",
    )(xb, w1, w3)


def _proj_ln_body(*refs, r, nk, nk_first, rows, ncols, emit_bf16):
    two_lhs = nk_first < nk
    ya_ref = refs[0]
    yb_ref = refs[1] if two_lhs else None
    w_ref, x_ref, g_ref, b_ref, o32_ref = refs[1 + two_lhs:6 + two_lhs]
    o16_ref = refs[6 + two_lhs] if emit_bf16 else None
    k = pl.program_id(1)

    @pl.when(k == 0)
    def _():
        o32_ref[...] = jnp.zeros(o32_ref.shape, F32)

    y = ya_ref[...]
    if two_lhs:
        y = jnp.where(k < nk_first, y, yb_ref[...])
    d = o32_ref.shape[1]
    for c0 in range(0, d, ncols):
        o32_ref[:, pl.ds(c0, ncols)] += jnp.dot(y, w_ref[:, pl.ds(c0, ncols)],
                                                preferred_element_type=F32)

    @pl.when(k == nk - 1)
    def _():
        g = g_ref[...]
        b = b_ref[...]
        tm = o32_ref.shape[0]

        def chunk(c, carry):
            r0 = pl.multiple_of(c * rows, rows)
            z = ALPHA * x_ref[pl.ds(r0, rows), :] + r * o32_ref[pl.ds(r0, rows), :]
            mu = jnp.mean(z, axis=-1, keepdims=True)
            zc = z - mu
            var = jnp.mean(zc * zc, axis=-1, keepdims=True)
            out = zc * lax.rsqrt(var + LN_EPS) * g + b
            o32_ref[pl.ds(r0, rows), :] = out
            if emit_bf16:
                o16_ref[pl.ds(r0, rows), :] = out.astype(BF16)
            return carry

        lax.fori_loop(0, tm // rows, chunk, 0)


def _proj_ln(ys, w, x, g, b, r, emit_bf16=True):
    s = ys[0].shape[0]
    kdim, d = w.shape
    tm = _pick(s, 512, SUBLANES)
    tk = _pick(math.gcd(*[y.shape[1] for y in ys]), 512, LANES)
    nk = kdim // tk
    nk_first = ys[0].shape[1] // tk
    y_specs = [pl.BlockSpec((tm, tk), lambda i, k: (i, jnp.minimum(k, nk_first - 1)))]
    if len(ys) == 2:
        y_specs.append(pl.BlockSpec((tm, tk), lambda i, k: (i, jnp.maximum(k - nk_first, 0))))
    rows = _pick(tm, LN_ROWS, SUBLANES)
    ncols = _pick(d, 1024, LANES)
    out_shape = [jax.ShapeDtypeStruct((s, d), F32)]
    out_specs = [pl.BlockSpec((tm, d), lambda i, k: (i, 0))]
    if emit_bf16:
        out_shape.append(jax.ShapeDtypeStruct((s, d), BF16))
        out_specs.append(pl.BlockSpec((tm, d), lambda i, k: (i, 0)))
    outs = pl.pallas_call(
        functools.partial(_proj_ln_body, r=r, nk=nk, nk_first=nk_first, rows=rows, ncols=ncols,
                          emit_bf16=emit_bf16),
        grid=(s // tm, nk),
        in_specs=y_specs + [pl.BlockSpec((tk, d), lambda i, k: (k, 0)),
                            pl.BlockSpec((tm, d), lambda i, k: (i, 0)),
                            pl.BlockSpec((1, d), lambda i, k: (0, 0)),
                            pl.BlockSpec((1, d), lambda i, k: (0, 0))],
        out_specs=out_specs,
        out_shape=out_shape,
        compiler_params=_cparams("parallel", "arbitrary"),
        name="proj_residual_ln",
    )(*ys, w, x, g.reshape(1, d), b.reshape(1, d))
    return outs if emit_bf16 else (outs[0], None)


def _proj_body(x_ref, w_ref, o_ref, *, scale, scaled_blocks, ncols):
    x = x_ref[...]
    tn = o_ref.shape[1]
    for c0 in range(0, tn, ncols):
        acc = jnp.dot(x, w_ref[:, pl.ds(c0, ncols)].astype(BF16), preferred_element_type=F32)
        if scaled_blocks:
            acc = acc * jnp.where(pl.program_id(1) < scaled_blocks, scale, 1.0).astype(F32)
        o_ref[:, pl.ds(c0, ncols)] = acc.astype(o_ref.dtype)


def _proj(xb, w, out_dtype, tn_pref, scale=1.0, scaled_cols=0, n=None):
    s, d = xb.shape
    n = w.shape[1] if n is None else n
    tm = _pick(s, 1024, SUBLANES)
    tn = _pick(n, tn_pref, LANES)
    if scaled_cols:
        tn = math.gcd(tn, scaled_cols)
    return pl.pallas_call(
        functools.partial(_proj_body, scale=scale, scaled_blocks=scaled_cols // tn,
                          ncols=_pick(tn, 256, LANES) if w.dtype == F32 else tn),
        grid=(s // tm, n // tn),
        in_specs=[pl.BlockSpec((tm, d), lambda i, j: (i, 0)),
                  pl.BlockSpec((d, tn), lambda i, j: (0, j))],
        out_specs=pl.BlockSpec((tm, tn), lambda i, j: (i, j)),
        out_shape=jax.ShapeDtypeStruct((s, n), out_dtype),
        compiler_params=_cparams("parallel", "arbitrary",
                                 vmem_limit_bytes=V7X_VMEM_LIMIT_F32_WEIGHTS_BYTES
                                 if w.dtype == F32 else V7X_VMEM_LIMIT_BYTES),
        name="proj",
    )(xb, w)


def _gates_body(f_ref, b_ref, o_ref, carry_ref, *, heads):
    i = pl.program_id(0)
    tm = f_ref.shape[0]

    @pl.when(i == 0)
    def _():
        carry_ref[...] = jnp.zeros_like(carry_ref)

    lane = lax.broadcasted_iota(jnp.int32, (tm, LANES), 1)
    xv = f_ref[...] + b_ref[...]
    logf = jnp.minimum(xv, 0.0) - jnp.log1p(jnp.exp(-jnp.abs(xv)))
    logf = jnp.where(lane < heads, logf, 0.0)
    row = lax.broadcasted_iota(jnp.int32, (tm, tm), 0)
    col = lax.broadcasted_iota(jnp.int32, (tm, tm), 1)
    tril = jnp.where(row >= col, 1.0, 0.0).astype(BF16)
    hi, mid, lo = _split3(logf)
    c = (jnp.dot(tril, hi, preferred_element_type=F32)
         + jnp.dot(tril, mid, preferred_element_type=F32)
         + jnp.dot(tril, lo, preferred_element_type=F32)) + carry_ref[...]
    carry_ref[...] = c[tm - 1:tm, :]
    nhi, nmid, nlo = _split3(-LOG2E * c)
    packed = jnp.where(lane < GATE_LANE_STRIDE, nhi.astype(F32),
                       jnp.where(lane < 2 * GATE_LANE_STRIDE,
                                 pltpu.roll(nmid.astype(F32), GATE_LANE_STRIDE, 1),
                                 pltpu.roll(nlo.astype(F32), 2 * GATE_LANE_STRIDE, 1)))
    o_ref[...] = packed.astype(BF16)


def _fox_gates(uf, ucols, b_f_pad, heads):
    s = uf.shape[0]
    tm = _pick(s, 512, SUBLANES)
    fblk = ucols // LANES
    return pl.pallas_call(
        functools.partial(_gates_body, heads=heads),
        grid=(s // tm,),
        in_specs=[pl.BlockSpec((tm, LANES), lambda i: (i, fblk)),
                  pl.BlockSpec((1, LANES), lambda i: (0, 0))],
        out_specs=pl.BlockSpec((tm, LANES), lambda i: (i, 0)),
        out_shape=jax.ShapeDtypeStruct((s, LANES), BF16),
        scratch_shapes=[pltpu.VMEM((1, LANES), F32)],
        compiler_params=_cparams("arbitrary"),
        name="fox_gates",
    )(uf, b_f_pad)


def _flash_body(*refs, tb, fox, rchunk):
    n_in = 4 if fox else 5
    q1_ref = refs[0]
    q2_ref = None if fox else refs[1]
    k1_ref, k2_ref, v_ref = refs[n_in - 3:n_in]
    o_ref = refs[n_in]
    q_sc, s0, s1, p0, p1, a0, a1, m_sc, acc_sc = refs[n_in + 1:]
    s_bufs, p_bufs, a_bufs = (s0, s1), (p0, p1), (a0, a1)
    h = pl.program_id(0)
    g = pl.program_id(1)

    def load_q(t):
        q_sc[:, pl.ds(0, HEAD_DIM)] = q1_ref[pl.ds(t * tb, tb), :]
        if not fox:
            q_sc[:, pl.ds(HEAD_DIM, LANES)] = q2_ref[pl.ds(t * tb, tb), :]

    def reset_state():
        m_sc[...] = jnp.full(m_sc.shape, -jnp.inf, F32)
        acc_sc[...] = jnp.zeros(acc_sc.shape, F32)

    def write_out(t):
        o_ref[pl.ds(t * tb, tb), :] = (acc_sc[:, pl.ds(0, HEAD_DIM)]
                                       / acc_sc[:, pl.ds(HEAD_DIM, LANES)]).astype(o_ref.dtype)

    if fox:
        lane = lax.broadcasted_iota(jnp.int32, (tb, LANES), 1)
        sel = (lane < 3 * GATE_LANE_STRIDE) & (jnp.bitwise_and(lane, GATE_LANE_STRIDE - 1) == h)
        q_sc[:, pl.ds(HEAD_DIM, LANES)] = jnp.where(sel, 1.0, 0.0).astype(BF16)
    load_q(0)
    reset_state()
    p1[...] = jnp.zeros(p1.shape, BF16)
    a1[...] = jnp.ones(a1.shape, F32)
    ones = jnp.ones((tb, LANES), BF16)
    hb = tb // 2

    def qk(b, s_dst):
        k0 = pl.multiple_of(b * tb, tb)
        kk = jnp.concatenate([k1_ref[pl.ds(k0, tb), :], k2_ref[pl.ds(k0, tb), :]], axis=1)
        s_dst[...] = lax.dot_general(q_sc[...], kk, (((1,), (1,)), ((), ())),
                                     preferred_element_type=F32)

    def pv(b, p_src, a_src):
        k0 = pl.multiple_of(jnp.maximum(b, 0) * tb, tb)
        vv = jnp.concatenate([v_ref[pl.ds(k0, tb), :], ones], axis=1)
        acc_sc[...] = a_src[...] * acc_sc[...] + jnp.dot(p_src[...], vv, preferred_element_type=F32)

    def sm(s_src, p_dst, a_dst, diagonal):
        for r0 in range(0, tb, rchunk):
            ncol = hb if diagonal and r0 + rchunk <= hb else tb
            s = s_src[pl.ds(r0, rchunk), pl.ds(0, ncol)]
            if diagonal:
                row = r0 + lax.broadcasted_iota(jnp.int32, (rchunk, ncol), 0)
                col = lax.broadcasted_iota(jnp.int32, (rchunk, ncol), 1)
                s = jnp.where(row >= col, s, -jnp.inf)
            m_prev = m_sc[pl.ds(r0, rchunk), :]
            m_new = jnp.maximum(m_prev, jnp.max(s, axis=1, keepdims=True))
            a_dst[pl.ds(r0, rchunk), :] = jnp.exp2(m_prev - m_new)
            p_dst[pl.ds(r0, rchunk), pl.ds(0, ncol)] = jnp.exp2(s - m_new).astype(BF16)
            m_sc[pl.ds(r0, rchunk), :] = m_new

    def pv_diagonal(b, p_src, a_src):
        k0 = pl.multiple_of(b * tb, tb)
        vv = jnp.concatenate([v_ref[pl.ds(k0, tb), :], ones], axis=1)
        acc_sc[pl.ds(0, hb), :] = (a_src[pl.ds(0, hb), :] * acc_sc[pl.ds(0, hb), :]
                                   + jnp.dot(p_src[pl.ds(0, hb), pl.ds(0, hb)], vv[:hb],
                                             preferred_element_type=F32))
        acc_sc[pl.ds(hb, hb), :] = (a_src[pl.ds(hb, hb), :] * acc_sc[pl.ds(hb, hb), :]
                                    + jnp.dot(p_src[pl.ds(hb, hb), :], vv,
                                              preferred_element_type=F32))

    def half(b, cur):
        nxt = 1 - cur
        pv(b - 1, p_bufs[nxt], a_bufs[nxt])
        qk(b + 1, s_bufs[nxt])
        sm(s_bufs[cur], p_bufs[cur], a_bufs[cur], False)

    qk(0, s0)

    def pair_a(i, carry):
        half(2 * i, 0)
        half(2 * i + 1, 1)
        return carry

    lax.fori_loop(0, g, pair_a, 0)

    pv(2 * g - 1, p1, a1)
    load_q(1)
    qk(0, s1)
    sm(s0, p0, a0, True)

    pv_diagonal(2 * g, p0, a0)
    write_out(0)
    reset_state()
    qk(1, s0)
    sm(s1, p1, a1, False)

    def pair_b(i, carry):
        half(2 * i + 1, 0)
        half(2 * i + 2, 1)
        return carry

    lax.fori_loop(0, g, pair_b, 0)
    pv(2 * g, p1, a1)
    sm(s0, p0, a0, True)
    pv_diagonal(2 * g + 1, p0, a0)
    write_out(1)


def _flash(q, q1_blk, q2_blk, kv, k1_blk, v_blk, k2, heads, tb_pref=FLASH_BLOCK):
    s = q.shape[0]
    tb = _pick(s, tb_pref, LANES)
    rchunk = _pick(tb, 64, SUBLANES)
    assert tb % (2 * LANES) == 0 and (tb // 2) % rchunk == 0 and s % (2 * tb) == 0
    fox = q2_blk is None
    in_specs = [pl.BlockSpec((2 * tb, HEAD_DIM), lambda h, i: (i, q1_blk + h))]
    args = [q]
    if not fox:
        in_specs.append(pl.BlockSpec((2 * tb, HEAD_DIM), lambda h, i: (i, q2_blk + h)))
        args.append(q)
    in_specs += [pl.BlockSpec((s, HEAD_DIM), lambda h, i: (0, k1_blk + h)),
                 pl.BlockSpec((s, LANES), lambda h, i: (0, 0)),
                 pl.BlockSpec((s, HEAD_DIM), lambda h, i: (0, v_blk + h))]
    args += [kv, k2, kv]
    score = pltpu.VMEM((tb, tb), F32)
    prob = pltpu.VMEM((tb, tb), BF16)
    col = pltpu.VMEM((tb, 1), F32)
    return pl.pallas_call(
        functools.partial(_flash_body, tb=tb, fox=fox, rchunk=rchunk),
        grid=(heads, s // (2 * tb)),
        in_specs=in_specs,
        out_specs=pl.BlockSpec((2 * tb, HEAD_DIM), lambda h, i: (i, h)),
        out_shape=jax.ShapeDtypeStruct((s, heads * HEAD_DIM), BF16),
        scratch_shapes=[pltpu.VMEM((tb, HEAD_DIM + LANES), BF16), score, score, prob, prob,
                        col, col, col, pltpu.VMEM((tb, HEAD_DIM + LANES), F32)],
        compiler_params=_cparams("parallel", "arbitrary"),
        name="flash_fox" if fox else "flash_mla",
    )(*args)


def _pool_body(u_ref, halo_ref, w_ref, sc_ref, o_ref, buf, *, groups, gdim):
    i = pl.program_id(0)
    tm = u_ref.shape[0]
    buf[pl.ds(0, POOL_HALO), :] = jnp.where(i == 0, 0.0, halo_ref[...])
    buf[pl.ds(POOL_HALO, tm), :] = u_ref[...]
    t = (i * tm + lax.broadcasted_iota(jnp.int32, (tm, 1), 0)).astype(F32)
    for g in range(groups):
        win = POOL_WINDOWS[g]
        c0 = g * gdim
        tot = buf[pl.ds(POOL_HALO, tm), pl.ds(c0, gdim)]
        cur = tot
        for lag in range(1, win):
            tot = tot + buf[pl.ds(POOL_HALO - lag, tm), pl.ds(c0, gdim)]
        count = jnp.minimum(t + 1.0, float(win))
        pooled = tot / count - cur
        mixed = jnp.dot(pooled.astype(BF16), w_ref[g], preferred_element_type=F32)
        o_ref[:, pl.ds(c0, gdim)] = (mixed * sc_ref[:, pl.ds(c0, gdim)]).astype(o_ref.dtype)


def _pool(uf, pool_w_b, pool_scale):
    s = uf.shape[0]
    groups, gdim, _ = pool_w_b.shape
    c = groups * gdim
    tm = _pick(s, 256, POOL_HALO)
    hb = tm // POOL_HALO
    return pl.pallas_call(
        functools.partial(_pool_body, groups=groups, gdim=gdim),
        grid=(s // tm,),
        in_specs=[pl.BlockSpec((tm, c), lambda i: (i, 0)),
                  pl.BlockSpec((POOL_HALO, c), lambda i: (jnp.maximum(i * hb - 1, 0), 0)),
                  pl.BlockSpec((groups, gdim, gdim), lambda i: (0, 0, 0)),
                  pl.BlockSpec((1, c), lambda i: (0, 0))],
        out_specs=pl.BlockSpec((tm, c), lambda i: (i, 0)),
        out_shape=jax.ShapeDtypeStruct((s, c), BF16),
        scratch_shapes=[pltpu.VMEM((tm + POOL_HALO, c), F32)],
        compiler_params=_cparams("parallel"),
        name="multiscale_pool",
    )(uf, uf, pool_w_b, pool_scale.reshape(1, c))


def _conv_body(a_ref, g_ref, ha_ref, hg_ref, w_ref, cb_ref, lg_ref, lb_ref, o_ref, buf, sh, ybuf, *, rows, lanes):
    i = pl.program_id(0)
    tm, c = a_ref.shape
    halo = ha_ref[...] * jax.nn.sigmoid(hg_ref[...])
    buf[pl.ds(0, CONV_HALO), :] = jnp.where(i == 0, 0.0, halo)
    buf[pl.ds(CONV_HALO, tm), :] = a_ref[...] * jax.nn.sigmoid(g_ref[...])
    span = sh.shape[1]
    for b in range(1, SUBLANES):
        sh[b - 1] = buf[pl.ds(b, span), :]
    base = CONV_HALO - (CONV_TAPS - 1)
    for r0 in range(0, tm, rows):
        for c0 in range(0, c, lanes):
            acc = jnp.zeros((rows, lanes), F32) + cb_ref[:, pl.ds(c0, lanes)]
            for j in range(CONV_TAPS):
                hi, lo = divmod(base + j, SUBLANES)
                if lo == 0:
                    win = buf[pl.ds(r0 + SUBLANES * hi, rows), pl.ds(c0, lanes)]
                else:
                    win = sh[lo - 1, pl.ds(r0 + SUBLANES * hi, rows), pl.ds(c0, lanes)]
                acc = acc + w_ref[pl.ds(j, 1), pl.ds(c0, lanes)] * win
            ybuf[pl.ds(r0, rows), pl.ds(c0, lanes)] = acc
    y = ybuf[...]
    mu = jnp.mean(y, axis=-1, keepdims=True)
    yc = y - mu
    var = jnp.mean(yc * yc, axis=-1, keepdims=True)
    z = yc * lax.rsqrt(var + LN_EPS) * lg_ref[...] + lb_ref[...]
    o_ref[...] = (z * jax.nn.sigmoid(z)).astype(o_ref.dtype)


def _conformer_conv(h1, cw, conv_w_p, conv_b, ln_g, ln_b):
    s = h1.shape[0]
    tm = _pick(s, 128, CONV_HALO)
    hb = tm // CONV_HALO
    rows = _pick(tm, 32, SUBLANES)
    lanes = _pick(cw, 256, LANES)
    nblk = 1
    vec = lambda a: a.reshape(1, cw)
    return pl.pallas_call(
        functools.partial(_conv_body, rows=rows, lanes=lanes),
        grid=(s // tm,),
        in_specs=[pl.BlockSpec((tm, cw), lambda i: (i, 0)),
                  pl.BlockSpec((tm, cw), lambda i: (i, nblk)),
                  pl.BlockSpec((CONV_HALO, cw), lambda i: (jnp.maximum(i * hb - 1, 0), 0)),
                  pl.BlockSpec((CONV_HALO, cw), lambda i: (jnp.maximum(i * hb - 1, 0), nblk)),
                  pl.BlockSpec((CONV_HALO, cw), lambda i: (0, 0)),
                  pl.BlockSpec((1, cw), lambda i: (0, 0)),
                  pl.BlockSpec((1, cw), lambda i: (0, 0)),
                  pl.BlockSpec((1, cw), lambda i: (0, 0))],
        out_specs=pl.BlockSpec((tm, cw), lambda i: (i, 0)),
        out_shape=jax.ShapeDtypeStruct((s, cw), BF16),
        scratch_shapes=[pltpu.VMEM((tm + CONV_HALO, cw), F32),
                        pltpu.VMEM((SUBLANES - 1, tm + CONV_HALO - SUBLANES, cw), F32),
                        pltpu.VMEM((tm, cw), F32)],
        compiler_params=_cparams("parallel"),
        name="conformer_conv",
    )(h1, h1, h1, h1, conv_w_p, vec(conv_b), vec(ln_g), vec(ln_b))


def _rope128(r, cosv, sinv, lane):
    half = MLA_ROPE_DIM // 2
    s1 = jnp.where(lane < half, -sinv, 0.0)
    s2 = jnp.where((lane >= half) & (lane < 2 * half), sinv, 0.0)
    return r * cosv + pltpu.roll(r, LANES - half, 1) * s1 + pltpu.roll(r, half, 1) * s2


def _rope_k_body(pos_ref, invf_ref, kpe_ref, cos_ref, sin_ref, k2_ref):
    tm = pos_ref.shape[0]
    ang = pos_ref[...].astype(F32) * invf_ref[...]
    cosv = jnp.cos(ang)
    sinv = jnp.sin(ang)
    cos_ref[...] = cosv
    sin_ref[...] = sinv
    lane = lax.broadcasted_iota(jnp.int32, (tm, LANES), 1)
    k2_ref[...] = _rope128(kpe_ref[...], cosv, sinv, lane).astype(BF16)


def _rope_k(positions_col, invf, h1, kpe_block):
    s = h1.shape[0]
    tm = _pick(s, 512, SUBLANES)
    tab = jax.ShapeDtypeStruct((s, LANES), F32)
    return pl.pallas_call(
        _rope_k_body,
        grid=(s // tm,),
        in_specs=[pl.BlockSpec((tm, 1), lambda i: (i, 0)),
                  pl.BlockSpec((1, LANES), lambda i: (0, 0)),
                  pl.BlockSpec((tm, LANES), lambda i: (i, kpe_block))],
        out_specs=[pl.BlockSpec((tm, LANES), lambda i: (i, 0))] * 3,
        out_shape=[tab, tab, jax.ShapeDtypeStruct((s, LANES), BF16)],
        compiler_params=_cparams("parallel"),
        name="rope_tables_key",
    )(positions_col, invf, h1)


def _rms_proj_body(c_ref, g_ref, w_ref, cos_ref, sin_ref, o_ref, xn_ref, *, scale, rope_from):
    j = pl.program_id(1)
    tm, tn = o_ref.shape

    @pl.when(j == 0)
    def _():
        xf = c_ref[...]
        ms = jnp.mean(xf * xf, axis=-1, keepdims=True)
        xn_ref[...] = (xf * lax.rsqrt(ms + RMS_EPS) * g_ref[...]).astype(BF16)

    acc = jnp.dot(xn_ref[...], w_ref[...], preferred_element_type=F32)

    def plain():
        o_ref[...] = (acc * scale).astype(o_ref.dtype)

    if rope_from is None:
        plain()
        return

    pl.when(j < rope_from)(plain)

    @pl.when(j >= rope_from)
    def _():
        lane = lax.broadcasted_iota(jnp.int32, (tm, LANES), 1)
        cosv = cos_ref[...]
        sinv = sin_ref[...]
        for c0 in range(0, tn, LANES):
            r = _rope128(acc[:, c0:c0 + LANES], cosv, sinv, lane)
            o_ref[:, pl.ds(c0, LANES)] = (r * scale).astype(o_ref.dtype)


def _rms_proj(h1, cblock, rank, g, w, cosv, sinv, scale, rope_cols):
    s = h1.shape[0]
    n = w.shape[1]
    tm = _pick(s, 1024, SUBLANES)
    tn = _pick(n, 1024, LANES)
    if rope_cols:
        tn = math.gcd(tn, n - rope_cols)
    rope_from = (n - rope_cols) // tn if rope_cols else None
    return pl.pallas_call(
        functools.partial(_rms_proj_body, scale=scale, rope_from=rope_from),
        grid=(s // tm, n // tn),
        in_specs=[pl.BlockSpec((tm, rank), lambda i, j: (i, cblock)),
                  pl.BlockSpec((1, rank), lambda i, j: (0, 0)),
                  pl.BlockSpec((rank, tn), lambda i, j: (0, j)),
                  pl.BlockSpec((tm, LANES), lambda i, j: (i, 0)),
                  pl.BlockSpec((tm, LANES), lambda i, j: (i, 0))],
        out_specs=pl.BlockSpec((tm, tn), lambda i, j: (i, j)),
        out_shape=jax.ShapeDtypeStruct((s, n), BF16),
        scratch_shapes=[pltpu.VMEM((tm, rank), BF16)],
        compiler_params=_cparams("parallel", "arbitrary"),
        name="rms_up_proj",
    )(h1, g.reshape(1, rank), w, cosv, sinv)


def _ffn(x32, x16, w1, w3, w2, g, b, emit_bf16=True):
    d, f = w1.shape
    fp = -(-f // 512) * 512
    w2p = _to_bf16(w2, fp)
    hidden = _ffn_gateup(x16, w1, w3, fp)
    return _proj_ln((hidden,), w2p, x32, g, b, 0.5, emit_bf16)


def _mixer_fox_pool(x32, x16, w_in, b_f, pool_w, pool_scale, w_out, g, b):
    d = x32.shape[1]
    heads = b_f.shape[0]
    fw = heads * HEAD_DIM
    groups, gdim, _ = pool_w.shape
    pw = groups * gdim
    w_uf = jnp.concatenate([w_in[:, 3 * fw + heads:], w_in[:, 3 * fw:3 * fw + heads],
                            jnp.zeros((d, LANES - heads), F32)], axis=1).astype(BF16)
    b_f_pad = jnp.pad(b_f, (0, LANES - heads)).reshape(1, LANES)

    qkv = _proj(x16, w_in.astype(BF16), BF16, 1024, scale=LOG2E * HEAD_DIM ** -0.5, scaled_cols=fw,
                n=3 * fw)
    uf = _proj(x16, w_uf, F32, pw + LANES)
    gate_parts = _fox_gates(uf, pw, b_f_pad, heads)
    y_a = _flash(qkv, 0, None, qkv, heads, 2 * heads, gate_parts, heads)
    y_b = _pool(uf, pool_w.astype(BF16), pool_scale)
    return _proj_ln((y_a, y_b), w_out.astype(BF16), x32, g, b, 1.0)


def _mixer_conv_mla(x32, x16, positions, w_in, conv_w, conv_b, conv_ln_g, conv_ln_b,
                    q_norm_g, w_uq, kv_norm_g, w_ukv, w_out, g, b):
    s, d = x32.shape
    cw = conv_b.shape[0]
    q_rank = q_norm_g.shape[0]
    kv_rank = kv_norm_g.shape[0]
    qk_dim = HEAD_DIM + MLA_ROPE_DIM
    heads = w_uq.shape[1] // qk_dim
    hw = heads * HEAD_DIM
    n_in = w_in.shape[1]
    n_in_p = -(-(n_in + MLA_ROPE_DIM) // 768) * 768
    w_in_p = jnp.pad(w_in.astype(BF16), ((0, 0), (0, n_in_p - n_in)))
    c_q_off = 2 * cw
    c_kv_off = c_q_off + q_rank
    kpe_off = c_kv_off + kv_rank
    assert c_q_off % q_rank == 0 and c_kv_off % kv_rank == 0 and kpe_off % LANES == 0

    wq = w_uq.reshape(q_rank, heads, qk_dim)
    wq_rope = jnp.pad(wq[:, :, HEAD_DIM:], ((0, 0), (0, 0), (0, LANES - MLA_ROPE_DIM)))
    wq_p = jnp.concatenate([wq[:, :, :HEAD_DIM].reshape(q_rank, hw),
                            wq_rope.reshape(q_rank, heads * LANES)], axis=1).astype(BF16)
    wkv = w_ukv.reshape(kv_rank, heads, 2 * HEAD_DIM)
    wkv_p = jnp.concatenate([wkv[:, :, :HEAD_DIM].reshape(kv_rank, hw),
                             wkv[:, :, HEAD_DIM:].reshape(kv_rank, hw)], axis=1).astype(BF16)
    conv_w_p = jnp.pad(conv_w.reshape(CONV_TAPS, cw), ((0, CONV_HALO - CONV_TAPS), (0, 0)))
    half = MLA_ROPE_DIM // 2
    invf = np.zeros((1, LANES), np.float32)
    freqs = ROPE_BASE ** (-np.arange(half, dtype=np.float32) / half)
    invf[0, :half] = freqs
    invf[0, half:2 * half] = freqs
    invf = jnp.asarray(invf)

    h1 = _proj(x16, w_in_p, F32, 768)
    y_c = _conformer_conv(h1, cw, conv_w_p, conv_b, conv_ln_g, conv_ln_b)
    cosv, sinv, k2 = _rope_k(positions.reshape(s, 1), invf, h1, kpe_off // LANES)
    q = _rms_proj(h1, c_q_off // q_rank, q_rank, q_norm_g, wq_p, cosv, sinv,
                  LOG2E * qk_dim ** -0.5, heads * LANES)
    kv = _rms_proj(h1, c_kv_off // kv_rank, kv_rank, kv_norm_g, wkv_p, cosv, sinv, 1.0, 0)
    y_d = _flash(q, 0, heads, kv, 0, heads, k2, heads)
    return _proj_ln((y_c, y_d), w_out.astype(BF16), x32, g, b, 1.0)


def kernel(x, positions, l0_ffn1_w1, l0_ffn1_w3, l0_ffn1_w2, l0_ln_ffn1_g, l0_ln_ffn1_b, l0_w_in, l0_b_f, l0_pool_w, l0_pool_scale, l0_w_out, l0_ln_mix_g, l0_ln_mix_b, l0_ffn2_w1, l0_ffn2_w3, l0_ffn2_w2, l0_ln_ffn2_g, l0_ln_ffn2_b, l1_ffn1_w1, l1_ffn1_w3, l1_ffn1_w2, l1_ln_ffn1_g, l1_ln_ffn1_b, l1_w_in, l1_conv_w, l1_conv_b, l1_conv_ln_g, l1_conv_ln_b, l1_q_norm_g, l1_w_uq, l1_kv_norm_g, l1_w_ukv, l1_w_out, l1_ln_mix_g, l1_ln_mix_b, l1_ffn2_w1, l1_ffn2_w3, l1_ffn2_w2, l1_ln_ffn2_g, l1_ln_ffn2_b):
    bsz, s, d = x.shape
    outs = []
    for bi in range(bsz):
        x32 = x.reshape(s, d) if bsz == 1 else x[bi]
        x16 = _to_bf16(x32)
        pos = positions.reshape(s) if bsz == 1 else positions[bi]
        x32, x16 = _ffn(x32, x16, l0_ffn1_w1, l0_ffn1_w3, l0_ffn1_w2, l0_ln_ffn1_g, l0_ln_ffn1_b)
        x32, x16 = _mixer_fox_pool(x32, x16, l0_w_in, l0_b_f, l0_pool_w, l0_pool_scale, l0_w_out,
                                   l0_ln_mix_g, l0_ln_mix_b)
        x32, x16 = _ffn(x32, x16, l0_ffn2_w1, l0_ffn2_w3, l0_ffn2_w2, l0_ln_ffn2_g, l0_ln_ffn2_b)
        x32, x16 = _ffn(x32, x16, l1_ffn1_w1, l1_ffn1_w3, l1_ffn1_w2, l1_ln_ffn1_g, l1_ln_ffn1_b)
        x32, x16 = _mixer_conv_mla(x32, x16, pos, l1_w_in, l1_conv_w, l1_conv_b, l1_conv_ln_g,
                                   l1_conv_ln_b, l1_q_norm_g, l1_w_uq, l1_kv_norm_g, l1_w_ukv,
                                   l1_w_out, l1_ln_mix_g, l1_ln_mix_b)
        x32, _ = _ffn(x32, x16, l1_ffn2_w1, l1_ffn2_w3, l1_ffn2_w2, l1_ln_ffn2_g, l1_ln_ffn2_b,
                      emit_bf16=False)
        outs.append(x32)
    return outs[0].reshape(1, s, d) if bsz == 1 else jnp.stack(outs, axis=0)
```
